```python
import math
import jax, jax.numpy as jnp
from jax import lax
import numpy as np

D_MODEL = 1024
BATCH = 8
SEQ = 4096
DEPTH = 2
DEC_BATCH = 32
DEC_SEQ = 4
PAST_LEN = 16384
PAGE_SIZE = 128

SB_HEAD_DIM = 64
SB_WIDTH = D_MODEL // 2
SB_HEADS = SB_WIDTH // SB_HEAD_DIM
SB_BIAS_INIT = -8.0
POOL_WINDOWS = (2, 4, 8, 16)
POOL_WIDTH = D_MODEL - SB_WIDTH
POOL_GROUP = POOL_WIDTH // len(POOL_WINDOWS)
POOL_BUF = max(POOL_WINDOWS) - 1
MIX_WIDTH = SB_WIDTH + POOL_WIDTH
IN_WIDTH = 3 * SB_WIDTH + POOL_WIDTH
Q_BLOCK = 128
N_MEM = 256
MEM_HEADS = 4
MEM_HEAD_DIM = D_MODEL // MEM_HEADS
D_FF = 256 * ((8 * D_MODEL // 3 + 255) // 256)
CONV_WIDTH = 3
ALPHA = (2 * DEPTH) ** 0.25
BETA = (8 * DEPTH) ** -0.25
LN_EPS = 1e-5
SB_SCALE = 1.0 / math.sqrt(SB_HEAD_DIM)
MEM_SCALE = 1.0 / math.sqrt(MEM_HEAD_DIM)

kernel_name = "hymba_stickbreak_pool_deepnorm_step"


def layer_norm(x, g, b):
    xf = x.astype(jnp.float32)
    mu = jnp.mean(xf, axis=-1, keepdims=True)
    xc = xf - mu
    var = jnp.mean(xc * xc, axis=-1, keepdims=True)
    return (xc * lax.rsqrt(var + LN_EPS) * g.astype(jnp.float32) + b.astype(jnp.float32)).astype(x.dtype)


def sb_attend(q, k, v, bias, q_pos, k_pos):
    z = (jnp.einsum('bqhd,bkhd->bhqk', q, k).astype(jnp.float32) * SB_SCALE
         + bias.astype(jnp.float32)[None, :, None, None])
    mask = (k_pos[None, :] < q_pos[:, None])[None, None]
    log_beta = jax.nn.log_sigmoid(z)
    log_keep = jnp.where(mask, log_beta - z, 0.0)
    after = lax.cumsum(log_keep, axis=3, reverse=True) - log_keep
    w = jnp.where(mask, jnp.exp(log_beta + after), 0.0)
    return jnp.einsum('bhqk,bkhd->bqhd', w.astype(v.dtype), v)


def pool_mix(u, buf, start_pos, w_g, scale):
    T = u.shape[1]
    ext = jnp.concatenate([buf.astype(u.dtype), u], axis=1)
    cs = jnp.cumsum(ext.astype(jnp.float32), axis=1)
    cs = jnp.pad(cs, ((0, 0), (1, 0), (0, 0)))
    pos = start_pos + jnp.arange(T)
    uf = u.astype(jnp.float32)
    end = cs[:, POOL_BUF + 1:POOL_BUF + 1 + T]
    outs = []
    for g, w in enumerate(POOL_WINDOWS):
        c0, c1 = g * POOL_GROUP, (g + 1) * POOL_GROUP
        win = end[..., c0:c1] - cs[:, POOL_BUF + 1 - w:POOL_BUF + 1 - w + T, c0:c1]
        cnt = jnp.minimum(w, pos + 1).astype(jnp.float32)[None, :, None]
        d = (win / cnt - uf[..., c0:c1]).astype(u.dtype)
        outs.append(jnp.einsum('btc,cd->btd', d, w_g[g]))
    y = jnp.concatenate(outs, axis=-1) * scale
    return y, ext[:, -POOL_BUF:]


def causal_dwconv(h, buf, w, b):
    T = h.shape[1]
    ext = jnp.concatenate([buf.astype(h.dtype), h], axis=1)
    out = b + sum(ext[:, i:i + T] * w[i] for i in range(CONV_WIDTH))
    return out, ext[:, -(CONV_WIDTH - 1):]


def mem_attend(x, w_q, mem_k, mem_v):
    B, T, _ = x.shape
    q = (x @ w_q).reshape(B, T, MEM_HEADS, MEM_HEAD_DIM)
    s = jnp.einsum('bthd,bmhd->bhtm', q, mem_k).astype(jnp.float32) * MEM_SCALE
    p = jax.nn.softmax(s, axis=-1).astype(mem_v.dtype)
    return jnp.einsum('bhtm,bmhd->bthd', p, mem_v).reshape(B, T, D_MODEL)


def prompt_attn(q, k, v, bias):
    B, T = q.shape[:2]
    nb = T // Q_BLOCK
    qb = q.reshape(B, nb, Q_BLOCK, SB_HEADS, SB_HEAD_DIM).swapaxes(0, 1)
    k_pos = jnp.arange(T)

    def blk(args):
        qi, i = args
        return sb_attend(qi, k, v, bias, i * Q_BLOCK + jnp.arange(Q_BLOCK), k_pos)

    o = lax.map(blk, (qb, jnp.arange(nb)))
    return o.swapaxes(0, 1).reshape(B, T, SB_HEADS, SB_HEAD_DIM)


def setup_inputs(seed: int = 0) -> dict:
    key = jax.random.key(seed)
    ks = jax.random.split(key, 32)
    f32 = jnp.float32
    n_pages = PAST_LEN // PAGE_SIZE
    n_phys = (5 * DEC_BATCH * n_pages + 3) // 4

    def w(k, shape, fan_in, gain=1.0):
        return jax.random.normal(k, shape, f32) * (gain * fan_in ** -0.5)

    def gain(k):
        return 1.0 + 0.02 * jax.random.normal(k, (DEPTH, D_MODEL), f32)

    def bias(k, shape):
        return 0.02 * jax.random.normal(k, shape, f32)

    page_table = jax.random.permutation(ks[0], n_phys)[:DEC_BATCH * n_pages].reshape(DEC_BATCH, n_pages).astype(jnp.int32)
    return {
        "x_prompt": jax.random.normal(ks[1], (BATCH, SEQ, D_MODEL), f32),
        "x_sample": jax.random.normal(ks[2], (DEC_BATCH, DEC_SEQ, D_MODEL), f32),
        "mem_prompt": jax.random.normal(ks[3], (BATCH, N_MEM, D_MODEL), f32),
        "cache_k": jax.random.normal(ks[4], (DEPTH, n_phys, PAGE_SIZE, SB_HEADS, SB_HEAD_DIM), f32),
        "cache_v": jax.random.normal(ks[5], (DEPTH, n_phys, PAGE_SIZE, SB_HEADS, SB_HEAD_DIM), f32),
        "page_table": page_table,
        "cache_mem_k": jax.random.normal(ks[6], (DEPTH, DEC_BATCH, N_MEM, MEM_HEADS, MEM_HEAD_DIM), f32),
        "cache_mem_v": jax.random.normal(ks[7], (DEPTH, DEC_BATCH, N_MEM, MEM_HEADS, MEM_HEAD_DIM), f32),
        "state_pool": jax.random.normal(ks[8], (DEPTH, DEC_BATCH, POOL_BUF, POOL_WIDTH), f32),
        "state_conv": jax.random.normal(ks[9], (DEPTH, DEC_BATCH, CONV_WIDTH - 1, 2 * D_FF), f32),
        "w_in": w(ks[10], (DEPTH, D_MODEL, IN_WIDTH), D_MODEL),
        "sb_bias": SB_BIAS_INIT + 0.1 * jax.random.normal(ks[28], (DEPTH, SB_HEADS), f32),
        "pool_w": w(ks[11], (DEPTH, len(POOL_WINDOWS), POOL_GROUP, POOL_GROUP), POOL_GROUP),
        "pool_scale": 1.0 + 0.1 * jax.random.normal(ks[12], (DEPTH, POOL_WIDTH), f32),
        "w_out": w(ks[13], (DEPTH, MIX_WIDTH, D_MODEL), MIX_WIDTH, BETA),
        "ln_mix_g": gain(ks[14]),
        "ln_mix_b": bias(ks[15], (DEPTH, D_MODEL)),
        "w_mq": w(ks[16], (DEPTH, D_MODEL, D_MODEL), D_MODEL),
        "w_mk": w(ks[17], (DEPTH, D_MODEL, D_MODEL), D_MODEL),
        "w_mv": w(ks[18], (DEPTH, D_MODEL, D_MODEL), D_MODEL),
        "w_mo": w(ks[19], (DEPTH, D_MODEL, D_MODEL), D_MODEL, BETA),
        "ln_mem_g": gain(ks[20]),
        "ln_mem_b": bias(ks[21], (DEPTH, D_MODEL)),
        "w_up": w(ks[22], (DEPTH, D_MODEL, 2 * D_FF), D_MODEL),
        "conv_w": w(ks[23], (DEPTH, CONV_WIDTH, 2 * D_FF), CONV_WIDTH),
        "conv_b": bias(ks[24], (DEPTH, 2 * D_FF)),
        "w_down": w(ks[25], (DEPTH, D_FF, D_MODEL), D_FF, BETA),
        "ln_ffn_g": gain(ks[26]),
        "ln_ffn_b": bias(ks[27], (DEPTH, D_MODEL)),
    }


def reference(x_prompt, x_sample, mem_prompt, cache_k, cache_v, page_table, cache_mem_k, cache_mem_v,
              state_pool, state_conv, w_in, sb_bias, pool_w, pool_scale, w_out, ln_mix_g, ln_mix_b,
              w_mq, w_mk, w_mv, w_mo, ln_mem_g, ln_mem_b, w_up, conv_w, conv_b, w_down,
              ln_ffn_g, ln_ffn_b):

    def layer(l, x, attn_fn, mem_k, mem_v, pool_buf, conv_buf, start_pos):
        B, T, _ = x.shape
        proj = x @ w_in[l]
        q = proj[..., :SB_WIDTH].reshape(B, T, SB_HEADS, SB_HEAD_DIM)
        k = proj[..., SB_WIDTH:2 * SB_WIDTH].reshape(B, T, SB_HEADS, SB_HEAD_DIM)
        v = proj[..., 2 * SB_WIDTH:3 * SB_WIDTH].reshape(B, T, SB_HEADS, SB_HEAD_DIM)
        u = proj[..., 3 * SB_WIDTH:]
        o_sb = attn_fn(q, k, v, sb_bias[l]).reshape(B, T, SB_WIDTH)
        o_pool, new_pool = pool_mix(u, pool_buf, start_pos, pool_w[l], pool_scale[l])
        mix = jnp.concatenate([o_sb, o_pool], axis=-1) @ w_out[l]
        x = layer_norm(ALPHA * x + mix, ln_mix_g[l], ln_mix_b[l])
        x = layer_norm(ALPHA * x + mem_attend(x, w_mq[l], mem_k, mem_v) @ w_mo[l], ln_mem_g[l], ln_mem_b[l])
        h, new_conv = causal_dwconv(x @ w_up[l], conv_buf, conv_w[l], conv_b[l])
        ff = (jax.nn.gelu(h[..., :D_FF]) * h[..., D_FF:]) @ w_down[l]
        x = layer_norm(ALPHA * x + ff, ln_ffn_g[l], ln_ffn_b[l])
        return x, k, v, new_pool, new_conv

    xp = x_prompt
    B = xp.shape[0]
    kp, vp, mkp, mvp, pp, cp = [], [], [], [], [], []
    for l in range(DEPTH):
        mem_k = (mem_prompt @ w_mk[l]).reshape(B, N_MEM, MEM_HEADS, MEM_HEAD_DIM)
        mem_v = (mem_prompt @ w_mv[l]).reshape(B, N_MEM, MEM_HEADS, MEM_HEAD_DIM)
        pool0 = jnp.zeros((B, POOL_BUF, POOL_WIDTH), xp.dtype)
        conv0 = jnp.zeros((B, CONV_WIDTH - 1, 2 * D_FF), xp.dtype)
        xp, k, v, npool, nconv = layer(l, xp, prompt_attn, mem_k, mem_v, pool0, conv0, 0)
        kp.append(k); vp.append(v); mkp.append(mem_k); mvp.append(mem_v); pp.append(npool); cp.append(nconv)

    xs = x_sample
    DB, T = xs.shape[:2]
    past = page_table.shape[1] * cache_k.shape[2]
    q_pos = past + jnp.arange(T)
    k_pos = jnp.arange(past + T)
    ks_, vs_, ps_, cs_ = [], [], [], []
    for l in range(DEPTH):
        past_k = cache_k[l][page_table].reshape(DB, past, SB_HEADS, SB_HEAD_DIM)
        past_v = cache_v[l][page_table].reshape(DB, past, SB_HEADS, SB_HEAD_DIM)

        def sample_attn(q, k, v, bias, past_k=past_k, past_v=past_v):
            k_all = jnp.concatenate([past_k.astype(k.dtype), k], axis=1)
            v_all = jnp.concatenate([past_v.astype(v.dtype), v], axis=1)
            return sb_attend(q, k_all, v_all, bias, q_pos, k_pos)

        xs, k, v, npool, nconv = layer(l, xs, sample_attn, cache_mem_k[l], cache_mem_v[l],
                                       state_pool[l], state_conv[l], past)
        ks_.append(k); vs_.append(v); ps_.append(npool); cs_.append(nconv)

    return (xp, xs,
            jnp.stack(kp), jnp.stack(vp), jnp.stack(mkp), jnp.stack(mvp), jnp.stack(pp), jnp.stack(cp),
            jnp.stack(ks_), jnp.stack(vs_), jnp.stack(ps_), jnp.stack(cs_))
```

```python
import functools
import math

import jax
import jax.numpy as jnp
from jax import lax
from jax.experimental import pallas as pl
from jax.experimental.pallas import tpu as pltpu

F32 = jnp.float32
BF16 = jnp.bfloat16

SB_HEAD_DIM = 64
POOL_WINDOWS = (2, 4, 8, 16)
POOL_BUF = max(POOL_WINDOWS) - 1
POOL_HALO = POOL_BUF + 1
N_MEM_HEADS = 4
CONV_WIDTH = 3
LN_EPS = 1e-5
SB_SCALE = 1.0 / math.sqrt(SB_HEAD_DIM)
LANES = 128
VMEM_LIMIT = 48 * 1024 * 1024


def _cparams(n_grid_dims):
    return pltpu.CompilerParams(
        dimension_semantics=("arbitrary",) * n_grid_dims,
        vmem_limit_bytes=VMEM_LIMIT)


def _const_spec(shape):
    nd = len(shape)
    return pl.BlockSpec(shape, lambda *_: (0,) * nd, pipeline_mode=pl.Buffered(1))


def _layer_norm(y, g, b):
    mu = jnp.mean(y, axis=-1, keepdims=True)
    yc = y - mu
    var = jnp.mean(yc * yc, axis=-1, keepdims=True)
    return yc * lax.rsqrt(var + LN_EPS) * g + b


def _dot(a, b):
    return jnp.dot(a, b, preferred_element_type=F32)


def _dot_nt(a, b):
    return lax.dot_general(a, b, (((1,), (1,)), ((), ())), preferred_element_type=F32)


def _mm_split_kernel(x_ref, w_ref, *o_refs, n_out, width):
    xb = x_ref[...].astype(BF16)
    for i in range(n_out):
        y = _dot(xb, w_ref[:, i * width:(i + 1) * width])
        o_refs[i][...] = y
        o_refs[n_out + i][...] = y.astype(BF16)


def _mm_split(x, w, n_out, tm):
    m, kdim = x.shape
    width = w.shape[1] // n_out
    o_spec = pl.BlockSpec((tm, width), lambda i: (i, 0))
    return pl.pallas_call(
        functools.partial(_mm_split_kernel, n_out=n_out, width=width),
        grid=(m // tm,),
        in_specs=[pl.BlockSpec((tm, kdim), lambda i: (i, 0)), _const_spec(w.shape)],
        out_specs=[o_spec] * (2 * n_out),
        out_shape=[jax.ShapeDtypeStruct((m, width), F32)] * n_out
        + [jax.ShapeDtypeStruct((m, width), BF16)] * n_out,
        compiler_params=_cparams(1),
        name="mm_split",
    )(x, w)


def _mm_kernel(x_ref, w_ref, o_ref):
    o_ref[...] = _dot(x_ref[...].astype(BF16), w_ref[...]).astype(o_ref.dtype)


def _mm(x, w, out_dtype, tm):
    m, kdim = x.shape
    n = w.shape[1]
    return pl.pallas_call(
        _mm_kernel,
        grid=(m // tm,),
        in_specs=[pl.BlockSpec((tm, kdim), lambda i: (i, 0)), _const_spec(w.shape)],
        out_specs=pl.BlockSpec((tm, n), lambda i: (i, 0)),
        out_shape=jax.ShapeDtypeStruct((m, n), out_dtype),
        compiler_params=_cparams(1),
        name="mm",
    )(x, w)


def _proj_in_kernel(x_ref, w_ref, q_ref, k_ref, v_ref, kb_ref, vb_ref, u_ref, *, sbw):
    xb = x_ref[...].astype(BF16)
    q = _dot(xb, w_ref[:, 0:sbw])
    q_ref[...] = (q * SB_SCALE).astype(BF16)
    k = _dot(xb, w_ref[:, sbw:2 * sbw])
    k_ref[...] = k
    kb_ref[...] = k.astype(BF16)
    v = _dot(xb, w_ref[:, 2 * sbw:3 * sbw])
    v_ref[...] = v
    vb_ref[...] = v.astype(BF16)
    u_ref[...] = _dot(xb, w_ref[:, 3 * sbw:])


def _proj_in(x, w, tm):
    m, d = x.shape
    sbw = d // 2
    pw = w.shape[1] - 3 * sbw
    row = lambda width: pl.BlockSpec((tm, width), lambda i: (i, 0))
    return pl.pallas_call(
        functools.partial(_proj_in_kernel, sbw=sbw),
        grid=(m // tm,),
        in_specs=[row(d), _const_spec(w.shape)],
        out_specs=[row(sbw), row(sbw), row(sbw), row(sbw), row(sbw), row(pw)],
        out_shape=[jax.ShapeDtypeStruct((m, sbw), BF16),
                   jax.ShapeDtypeStruct((m, sbw), F32),
                   jax.ShapeDtypeStruct((m, sbw), F32),
                   jax.ShapeDtypeStruct((m, sbw), BF16),
                   jax.ShapeDtypeStruct((m, sbw), BF16),
                   jax.ShapeDtypeStruct((m, pw), F32)],
        compiler_params=_cparams(1),
        name="proj_in",
    )(x, w)


def _pool_kernel(u_ref, buf_ref, w_ref, s_ref, o_ref, np_ref, halo_ref, *, tp, t_valid, start_pos):
    t = pl.program_id(1)

    @pl.when(t == 0)
    def _():
        halo_ref[...] = buf_ref[0]

    u = u_ref[0]
    ext = jnp.concatenate([halo_ref[...], u], axis=0)
    pos = start_pos + t * tp + lax.broadcasted_iota(jnp.int32, (tp, 1), 0)
    group = u.shape[1] // len(POOL_WINDOWS)
    for g, win in enumerate(POOL_WINDOWS):
        c0 = g * group
        s = ext[:, c0:c0 + group]
        span = 1
        while span < win:
            s = s + pltpu.roll(s, span, 0)
            span *= 2
        wsum = s[POOL_HALO:POOL_HALO + tp]
        inv_cnt = 1.0 / jnp.minimum(win, pos + 1).astype(F32)
        d = (wsum * inv_cnt - u[:, c0:c0 + group]).astype(BF16)
        y = _dot(d, w_ref[g]) * s_ref[:, c0:c0 + group]
        o_ref[0, :, c0:c0 + group] = y.astype(o_ref.dtype)
    tail = ext[t_valid:t_valid + POOL_HALO]
    halo_ref[...] = tail
    np_ref[0] = tail


def _pool(u, buf, w, scale, tp, t_valid, start_pos):
    b, t, width = u.shape
    return pl.pallas_call(
        functools.partial(_pool_kernel, tp=tp, t_valid=t_valid, start_pos=start_pos),
        grid=(b, t // tp),
        in_specs=[pl.BlockSpec((1, tp, width), lambda i, j: (i, j, 0)),
                  pl.BlockSpec((1, POOL_HALO, width), lambda i, j: (i, 0, 0)),
                  _const_spec(w.shape), _const_spec(scale.shape)],
        out_specs=[pl.BlockSpec((1, tp, width), lambda i, j: (i, j, 0)),
                   pl.BlockSpec((1, POOL_HALO, width), lambda i, j: (i, 0, 0))],
        out_shape=[jax.ShapeDtypeStruct((b, t, width), BF16),
                   jax.ShapeDtypeStruct((b, POOL_HALO, width), F32)],
        scratch_shapes=[pltpu.VMEM((POOL_HALO, width), F32)],
        compiler_params=_cparams(2),
        name="pool",
    )(u, buf, w, scale)


def _sb_tile(z, tri, carry, mask):
    sp = jnp.maximum(z, 0.0) + jnp.log(1.0 + jnp.exp(-jnp.abs(z)))
    lk = -sp
    if mask is not None:
        lk = jnp.where(mask, lk, 0.0)
    hi = lk.astype(BF16)
    lo = (lk - hi.astype(F32)).astype(BF16)
    suffix = _dot(hi, tri) + _dot(lo, tri)
    w = jnp.exp((z - sp) + (suffix - lk) + carry)
    if mask is not None:
        w = jnp.where(mask, w, 0.0)
    return w.astype(BF16), carry + suffix[:, 0:1]


def _suffix_matrix(n):
    r = lax.broadcasted_iota(jnp.int32, (n, n), 0)
    c = lax.broadcasted_iota(jnp.int32, (n, n), 1)
    return (r >= c).astype(BF16)


def _sb_prompt_kernel(bias_ref, q_ref, k_ref, v_ref, o_ref, *, tq):
    hp = pl.program_id(1)
    i = pl.program_id(2)
    lane = lax.broadcasted_iota(jnp.int32, (1, LANES), 1)
    first = lane < SB_HEAD_DIM
    q2 = q_ref[0]
    qz = jnp.zeros_like(q2)
    qm = (jnp.where(first, q2, qz), jnp.where(first, qz, q2))
    bias = (bias_ref[2 * hp], bias_ref[2 * hp + 1])
    tri = _suffix_matrix(tq)
    r = lax.broadcasted_iota(jnp.int32, (tq, tq), 0)
    c = lax.broadcasted_iota(jnp.int32, (tq, tq), 1)
    causal = c < r

    def tile(j, carry, mask):
        c0, c1, acc = carry
        start = pl.multiple_of(j * tq, tq)
        k2 = k_ref[0, pl.ds(start, tq), :]
        v2 = v_ref[0, pl.ds(start, tq), :]
        vz = jnp.zeros_like(v2)
        vm = (jnp.where(first, v2, vz), jnp.where(first, vz, v2))
        cs = [c0, c1]
        for h in range(2):
            z = _dot_nt(qm[h], k2) + bias[h]
            w, cs[h] = _sb_tile(z, tri, cs[h], mask)
            acc = acc + _dot(w, vm[h])
        return cs[0], cs[1], acc

    zero_c = jnp.zeros((tq, 1), F32)
    carry = tile(i, (zero_c, zero_c, jnp.zeros((tq, LANES), F32)), causal)
    carry = lax.fori_loop(0, i, lambda n, cr: tile(i - 1 - n, cr, None), carry)
    o_ref[0] = carry[2].astype(o_ref.dtype)


def _sb_prompt(q, k, v, bias, tq):
    b, t, width = q.shape
    return pl.pallas_call(
        functools.partial(_sb_prompt_kernel, tq=tq),
        grid=(b, width // LANES, t // tq),
        in_specs=[pl.BlockSpec(memory_space=pltpu.SMEM),
                  pl.BlockSpec((1, tq, LANES), lambda bi, hp, i: (bi, i, hp)),
                  pl.BlockSpec((1, t, LANES), lambda bi, hp, i: (bi, 0, hp)),
                  pl.BlockSpec((1, t, LANES), lambda bi, hp, i: (bi, 0, hp))],
        out_specs=pl.BlockSpec((1, tq, LANES), lambda bi, hp, i: (bi, i, hp)),
        out_shape=jax.ShapeDtypeStruct((b, t, width), BF16),
        compiler_params=_cparams(3),
        name="sb_prompt",
    )(bias, q, k, v)


def _sb_sample_kernel(pt_ref, bias_ref, q_ref, kn_ref, vn_ref, *rest, n_pg, n_heads, t_new):
    k_refs = rest[:n_pg]
    v_refs = rest[n_pg:2 * n_pg]
    o_ref = rest[2 * n_pg]
    acc_ref, c_ref = rest[2 * n_pg + 1:]
    g = pl.program_id(1)
    rows = t_new * n_heads
    width = q_ref.shape[2]
    page = kn_ref.shape[1]
    row_head = lax.broadcasted_iota(jnp.int32, (rows, 1), 0) % n_heads
    lane_head = lax.broadcasted_iota(jnp.int32, (1, width), 1) // SB_HEAD_DIM
    own = row_head == lane_head
    q2 = q_ref[0]
    qm = jnp.where(own, q2, jnp.zeros_like(q2))
    bias = jnp.zeros((rows, 1), F32)
    for h in range(n_heads):
        bias = jnp.where(row_head == h, bias_ref[h], bias)
    tri = _suffix_matrix(page)

    def attend(kb, vb, mask):
        z = _dot_nt(qm, kb) + bias
        w, c_new = _sb_tile(z, tri, c_ref[...], mask)
        c_ref[...] = c_new
        acc_ref[...] += _dot(w, vb)

    @pl.when(g == 0)
    def _():
        acc_ref[...] = jnp.zeros_like(acc_ref)
        c_ref[...] = jnp.zeros_like(c_ref)
        s_idx = lax.broadcasted_iota(jnp.int32, (rows, page), 1)
        t_idx = lax.broadcasted_iota(jnp.int32, (rows, page), 0) // n_heads
        attend(kn_ref[0], vn_ref[0], s_idx < t_idx)

    for p in reversed(range(n_pg)):
        attend(k_refs[p][0].astype(BF16), v_refs[p][0].astype(BF16), None)

    @pl.when(g == pl.num_programs(1) - 1)
    def _():
        o = jnp.where(own, acc_ref[...], 0.0).reshape(t_new, n_heads, width)
        o_ref[0] = jnp.sum(o, axis=1)


def _sb_sample(page_table, bias, q_rep, k_new, v_new, cache_k, cache_v, layer_base, n_pg, t_new):
    b, rows, width = q_rep.shape
    page = cache_k.shape[1]
    n_pages = page_table.shape[0] // b
    steps = n_pages // n_pg
    n_heads = rows // t_new

    def page_spec(p):
        def imap(bi, g, pt):
            return (layer_base + pt[bi * n_pages + (steps - 1 - g) * n_pg + p], 0, 0)
        return pl.BlockSpec((1, page, width), imap)

    per_b = lambda shape: pl.BlockSpec(shape, lambda bi, g, pt: (bi, 0, 0))
    grid_spec = pltpu.PrefetchScalarGridSpec(
        num_scalar_prefetch=1,
        grid=(b, steps),
        in_specs=[pl.BlockSpec(memory_space=pltpu.SMEM),
                  per_b((1, rows, width)), per_b((1, page, width)), per_b((1, page, width))]
        + [page_spec(p) for p in range(n_pg)] * 2,
        out_specs=per_b((1, t_new, width)),
        scratch_shapes=[pltpu.VMEM((rows, width), F32), pltpu.VMEM((rows, 1), F32)],
    )
    return pl.pallas_call(
        functools.partial(_sb_sample_kernel, n_pg=n_pg, n_heads=n_heads, t_new=t_new),
        grid_spec=grid_spec,
        out_shape=jax.ShapeDtypeStruct((b, t_new, width), F32),
        compiler_params=_cparams(2),
        name="sb_sample",
    )(page_table, bias, q_rep, k_new, v_new, *([cache_k] * n_pg), *([cache_v] * n_pg))


def _proj_ln_kernel(*refs, n_in, alpha):
    a_refs = refs[:n_in]
    w_refs = refs[n_in:2 * n_in]
    x_ref, g_ref, b_ref, o_ref = refs[2 * n_in:]
    y = alpha * x_ref[...]
    for a_ref, w_ref in zip(a_refs, w_refs):
        y = y + _dot(a_ref[...].astype(BF16), w_ref[...])
    o_ref[...] = _layer_norm(y, g_ref[...], b_ref[...])


def _proj_ln(a_list, w_list, x, g, b, alpha, tm):
    m, d = x.shape
    row = lambda width: pl.BlockSpec((tm, width), lambda i: (i, 0))
    return pl.pallas_call(
        functools.partial(_proj_ln_kernel, n_in=len(a_list), alpha=alpha),
        grid=(m // tm,),
        in_specs=[row(a.shape[1]) for a in a_list] + [_const_spec(w.shape) for w in w_list]
        + [row(d), _const_spec(g.shape), _const_spec(b.shape)],
        out_specs=row(d),
        out_shape=jax.ShapeDtypeStruct((m, d), F32),
        compiler_params=_cparams(1),
        name="proj_ln",
    )(*a_list, *w_list, x, g, b)


def _mem_core_kernel(q_ref, k_ref, v_ref, o_ref, *, scale):
    hd = q_ref.shape[2] // N_MEM_HEADS
    for h in range(N_MEM_HEADS):
        cols = slice(h * hd, (h + 1) * hd)
        s = _dot_nt(q_ref[0, :, cols].astype(BF16), k_ref[0, :, cols].astype(BF16)) * scale
        p = jnp.exp(s - jnp.max(s, axis=-1, keepdims=True))
        p = p * (1.0 / jnp.sum(p, axis=-1, keepdims=True))
        o_ref[0, :, cols] = _dot(p.astype(BF16), v_ref[0, :, cols].astype(BF16)).astype(o_ref.dtype)


def _mem_core(q, mem_k, mem_v, tm):
    b, t, d = q.shape
    n_mem = mem_k.shape[1]
    scale = 1.0 / math.sqrt(d // N_MEM_HEADS)
    return pl.pallas_call(
        functools.partial(_mem_core_kernel, scale=scale),
        grid=(b, t // tm),
        in_specs=[pl.BlockSpec((1, tm, d), lambda i, j: (i, j, 0)),
                  pl.BlockSpec((1, n_mem, d), lambda i, j: (i, 0, 0)),
                  pl.BlockSpec((1, n_mem, d), lambda i, j: (i, 0, 0))],
        out_specs=pl.BlockSpec((1, tm, d), lambda i, j: (i, j, 0)),
        out_shape=jax.ShapeDtypeStruct((b, t, d), q.dtype),
        compiler_params=_cparams(2),
        name="mem_core",
    )(q, mem_k, mem_v)


def _gelu_tanh(x):
    return 0.5 * x * (1.0 + jnp.tanh(math.sqrt(2.0 / math.pi) * (x + 0.044715 * (x * x * x))))


def _ffn_kernel(*refs, seq, d_ff, ck, alpha):
    if seq is None:
        x_ref, wu_ref, cw_ref, cb_ref, wd_ref, g_ref, b_ref, st_ref, o_ref, ns_ref, prev_ref = refs
        x = x_ref[0]
    else:
        x_ref, wu_ref, cw_ref, cb_ref, wd_ref, g_ref, b_ref, p1_ref, p2_ref, o_ref, h_ref = refs
        x = x_ref[...]
    tm = x.shape[0]
    n_prev = CONV_WIDTH - 1
    row = lax.broadcasted_iota(jnp.int32, (tm, 1), 0)
    if seq is None:
        @pl.when(pl.program_id(1) == 0)
        def _():
            prev_ref[...] = st_ref[0]
    else:
        row = row % seq

    xb = x.astype(BF16)
    y = alpha * x
    for c in range(d_ff // ck):
        conv = []
        for part in range(2):
            cols = slice(part * d_ff + c * ck, part * d_ff + (c + 1) * ck)
            h = _dot(xb, wu_ref[:, cols])
            if seq is None:
                p1 = prev_ref[1:2, cols]
                p2 = jnp.where(row == 1, p1, prev_ref[0:1, cols])
                prev_ref[:, cols] = h[tm - n_prev:]
                ns_ref[0, :, cols] = h[tm - n_prev:]
            else:
                p1 = p1_ref[:, cols]
                p2 = p2_ref[:, cols]
                h_ref[:, cols] = h
            h1 = jnp.where(row >= 1, pltpu.roll(h, 1, 0), p1)
            h2 = jnp.where(row >= 2, pltpu.roll(h, 2, 0), p2)
            conv.append(cb_ref[:, cols] + cw_ref[0:1, cols] * h2 + cw_ref[1:2, cols] * h1
                        + cw_ref[2:3, cols] * h)
        act = (_gelu_tanh(conv[0]) * conv[1]).astype(BF16)
        y = y + _dot(act, wd_ref[c * ck:(c + 1) * ck, :])
    out = _layer_norm(y, g_ref[...], b_ref[...])
    if seq is None:
        o_ref[0] = out
    else:
        o_ref[...] = out


def _ffn_prompt(x, w_up, conv_w, conv_b, w_down, g, b, state, alpha, tm, ck):
    bsz, t, d = x.shape
    d_ff = w_down.shape[0]
    consts = [w_up, conv_w, conv_b, w_down, g, b]
    return pl.pallas_call(
        functools.partial(_ffn_kernel, seq=None, d_ff=d_ff, ck=ck, alpha=alpha),
        grid=(bsz, t // tm),
        in_specs=[pl.BlockSpec((1, tm, d), lambda i, j: (i, j, 0))]
        + [_const_spec(a.shape) for a in consts]
        + [pl.BlockSpec((1,) + state.shape[1:], lambda i, j: (i, 0, 0))],
        out_specs=[pl.BlockSpec((1, tm, d), lambda i, j: (i, j, 0)),
                   pl.BlockSpec((1,) + state.shape[1:], lambda i, j: (i, 0, 0))],
        out_shape=[jax.ShapeDtypeStruct(x.shape, F32), jax.ShapeDtypeStruct(state.shape, F32)],
        scratch_shapes=[pltpu.VMEM(state.shape[1:], F32)],
        compiler_params=_cparams(2),
        name="ffn_prompt",
    )(x, *consts, state)


def _ffn_sample(x, w_up, conv_w, conv_b, w_down, g, b, prev1, prev2, seq, alpha, ck):
    m, d = x.shape
    d_ff = w_down.shape[0]
    args = [x, w_up, conv_w, conv_b, w_down, g, b, prev1, prev2]
    return pl.pallas_call(
        functools.partial(_ffn_kernel, seq=seq, d_ff=d_ff, ck=ck, alpha=alpha),
        grid=(1,),
        in_specs=[_const_spec(a.shape) for a in args],
        out_specs=[_const_spec((m, d)), _const_spec((m, 2 * d_ff))],
        out_shape=[jax.ShapeDtypeStruct((m, d), F32), jax.ShapeDtypeStruct((m, 2 * d_ff), F32)],
        compiler_params=_cparams(1),
        name="ffn_sample",
    )(*args)


def kernel(x_prompt, x_sample, mem_prompt, cache_k, cache_v, page_table, cache_mem_k, cache_mem_v, state_pool, state_conv, w_in, sb_bias, pool_w, pool_scale, w_out, ln_mix_g, ln_mix_b, w_mq, w_mk, w_mv, w_mo, ln_mem_g, ln_mem_b, w_up, conv_w, conv_b, w_down, ln_ffn_g, ln_ffn_b):
    depth = w_in.shape[0]
    bsz, seq, d = x_prompt.shape
    dbsz, dseq, _ = x_sample.shape
    sbw = d // 2
    n_heads = sbw // SB_HEAD_DIM
    n_mem = mem_prompt.shape[1]
    d_ff = w_down.shape[1]
    n_phys, page = cache_k.shape[1], cache_k.shape[2]
    alpha = (2 * depth) ** 0.25
    m_p = bsz * seq
    m_s = dbsz * dseq

    w_in_b, pool_w_b, w_out_b = w_in.astype(BF16), pool_w.astype(BF16), w_out.astype(BF16)
    w_mq_b, w_mo_b, w_up_b, w_down_b = (w_mq.astype(BF16), w_mo.astype(BF16), w_up.astype(BF16),
                                        w_down.astype(BF16))
    row2 = lambda a, l: a[l][None, :]

    w_mkv = jnp.concatenate([w_mk[l] for l in range(depth)] + [w_mv[l] for l in range(depth)],
                            axis=1).astype(BF16)
    mem_out = _mm_split(mem_prompt.reshape(bsz * n_mem, d), w_mkv, 2 * depth, tm=512)
    mem_f32, mem_b16 = mem_out[:2 * depth], mem_out[2 * depth:]

    def mixer_tail(l, x2d, o_sb, o_pool, mem_attend, tm, q_dtype):
        x1 = _proj_ln([o_sb, o_pool], [w_out_b[l, :sbw], w_out_b[l, sbw:]], x2d,
                      row2(ln_mix_g, l), row2(ln_mix_b, l), alpha, tm)
        qm = _mm(x1, w_mq_b[l], q_dtype, tm)
        om = mem_attend(qm)
        return _proj_ln([om], [w_mo_b[l]], x1, row2(ln_mem_g, l), row2(ln_mem_b, l), alpha, tm)

    xp = x_prompt.reshape(m_p, d)
    kp, vp, pp, cp = [], [], [], []
    pool0 = jnp.zeros((bsz, POOL_HALO, d - sbw), F32)
    conv0 = jnp.zeros((bsz, CONV_WIDTH - 1, 2 * d_ff), F32)
    for l in range(depth):
        q, k, v, kb, vb, u = _proj_in(xp, w_in_b[l], tm=512)
        to3 = lambda a: a.reshape(bsz, seq, a.shape[1])
        o_sb = _sb_prompt(to3(q), to3(kb), to3(vb), sb_bias[l], tq=256)
        o_pool, npool = _pool(to3(u), pool0, pool_w_b[l], row2(pool_scale, l), tp=512, t_valid=512, start_pos=0)
        mem_k = mem_b16[l].reshape(bsz, n_mem, d)
        mem_v = mem_b16[depth + l].reshape(bsz, n_mem, d)
        x2 = mixer_tail(l, xp, o_sb.reshape(m_p, sbw), o_pool.reshape(m_p, d - sbw),
                        lambda qm: _mem_core(qm.reshape(bsz, seq, d), mem_k, mem_v, tm=512).reshape(m_p, d),
                        tm=512, q_dtype=BF16)
        x3, nconv = _ffn_prompt(x2.reshape(bsz, seq, d), w_up_b[l], conv_w[l], row2(conv_b, l), w_down_b[l],
                                row2(ln_ffn_g, l), row2(ln_ffn_b, l), conv0, alpha, tm=512, ck=256)
        xp = x3.reshape(m_p, d)
        kp.append(k.reshape(bsz, seq, n_heads, SB_HEAD_DIM))
        vp.append(v.reshape(bsz, seq, n_heads, SB_HEAD_DIM))
        pp.append(npool[:, 1:])
        cp.append(nconv)
    mkp = [mem_f32[l].reshape(bsz, n_mem, N_MEM_HEADS, d // N_MEM_HEADS) for l in range(depth)]
    mvp = [mem_f32[depth + l].reshape(bsz, n_mem, N_MEM_HEADS, d // N_MEM_HEADS) for l in range(depth)]

    xs = x_sample.reshape(m_s, d)
    past = page_table.shape[1] * page
    pt_flat = page_table.reshape(-1)
    ck2 = cache_k.reshape(depth * n_phys, page, sbw)
    cv2 = cache_v.reshape(depth * n_phys, page, sbw)
    ks_, vs_, ps_, cs_ = [], [], [], []
    seq_pad = 8
    for l in range(depth):
        q, k, v, kb, vb, u = _proj_in(xs, w_in_b[l], tm=m_s)
        q_rep = jnp.repeat(q.reshape(dbsz, dseq, sbw), n_heads, axis=1)
        pad_keys = lambda a: jnp.pad(a.reshape(dbsz, dseq, sbw), ((0, 0), (0, page - dseq), (0, 0)))
        o_sb = _sb_sample(pt_flat, sb_bias[l], q_rep, pad_keys(kb), pad_keys(vb), ck2, cv2,
                          l * n_phys, n_pg=8, t_new=dseq)
        u_pad = jnp.pad(u.reshape(dbsz, dseq, d - sbw), ((0, 0), (0, seq_pad - dseq), (0, 0)))
        buf = jnp.pad(state_pool[l], ((0, 0), (1, 0), (0, 0)))
        o_pool, npool = _pool(u_pad, buf, pool_w_b[l], row2(pool_scale, l), tp=seq_pad, t_valid=dseq,
                              start_pos=past)
        o_pool = o_pool[:, :dseq].reshape(m_s, d - sbw)
        def mem_sample(qm, l=l):
            q_pad = jnp.pad(qm.reshape(dbsz, dseq, d), ((0, 0), (0, seq_pad - dseq), (0, 0)))
            om = _mem_core(q_pad, cache_mem_k[l].reshape(dbsz, n_mem, d),
                           cache_mem_v[l].reshape(dbsz, n_mem, d), tm=seq_pad)
            return om[:, :dseq].reshape(m_s, d)

        x2 = mixer_tail(l, xs, o_sb.reshape(m_s, sbw), o_pool, mem_sample, tm=m_s, q_dtype=F32)
        st = state_conv[l]
        zrow = jnp.zeros_like(st[:, :1])
        prev1 = jnp.concatenate([st[:, 1:2]] + [zrow] * (dseq - 1), axis=1).reshape(m_s, 2 * d_ff)
        prev2 = jnp.concatenate([st[:, 0:1], st[:, 1:2]] + [zrow] * (dseq - 2), axis=1).reshape(m_s, 2 * d_ff)
        xs, h = _ffn_sample(x2, w_up_b[l], conv_w[l], row2(conv_b, l), w_down_b[l],
                            row2(ln_ffn_g, l), row2(ln_ffn_b, l), prev1, prev2, dseq, alpha, ck=256)
        ks_.append(k.reshape(dbsz, dseq, n_heads, SB_HEAD_DIM))
        vs_.append(v.reshape(dbsz, dseq, n_heads, SB_HEAD_DIM))
        ps_.append(npool[:, 1:])
        cs_.append(h.reshape(dbsz, dseq, 2 * d_ff)[:, dseq - (CONV_WIDTH - 1):])

    return (xp.reshape(bsz, seq, d), xs.reshape(dbsz, dseq, d),
            jnp.stack(kp), jnp.stack(vp), jnp.stack(mkp), jnp.stack(mvp), jnp.stack(pp), jnp.stack(cp),
            jnp.stack(ks_), jnp.stack(vs_), jnp.stack(ps_), jnp.stack(cs_))
```

```python
import functools
import math

import jax
import jax.numpy as jnp
from jax import lax
from jax.experimental import pallas as pl
from jax.experimental.pallas import tpu as pltpu

F32 = jnp.float32
BF16 = jnp.bfloat16

SB_HEAD_DIM = 64
POOL_WINDOWS = (2, 4, 8, 16)
POOL_BUF = max(POOL_WINDOWS) - 1
POOL_HALO = POOL_BUF + 1
N_MEM_HEADS = 4
CONV_WIDTH = 3
LN_EPS = 1e-5
SB_SCALE = 1.0 / math.sqrt(SB_HEAD_DIM)
LANES = 128
CUM_BLOCK = 256
VMEM_LIMIT = 48 * 1024 * 1024


def _cparams(n_grid_dims):
    return pltpu.CompilerParams(
        dimension_semantics=("arbitrary",) * n_grid_dims,
        vmem_limit_bytes=VMEM_LIMIT)


def _const_spec(shape):
    nd = len(shape)
    return pl.BlockSpec(shape, lambda *_: (0,) * nd, pipeline_mode=pl.Buffered(1))


def _layer_norm(y, g, b):
    mu = jnp.mean(y, axis=-1, keepdims=True)
    yc = y - mu
    var = jnp.mean(yc * yc, axis=-1, keepdims=True)
    return yc * lax.rsqrt(var + LN_EPS) * g + b


def _dot(a, b):
    return jnp.dot(a, b, preferred_element_type=F32)


def _dot_nt(a, b):
    return lax.dot_general(a, b, (((1,), (1,)), ((), ())), preferred_element_type=F32)


def _mm_split_kernel(x_ref, w_ref, *o_refs, n_out, width):
    xb = x_ref[...].astype(BF16)
    for i in range(n_out):
        y = _dot(xb, w_ref[:, i * width:(i + 1) * width])
        o_refs[i][...] = y
        o_refs[n_out + i][...] = y.astype(BF16)


def _mm_split(x, w, n_out, tm):
    m, kdim = x.shape
    width = w.shape[1] // n_out
    o_spec = pl.BlockSpec((tm, width), lambda i: (i, 0))
    return pl.pallas_call(
        functools.partial(_mm_split_kernel, n_out=n_out, width=width),
        grid=(m // tm,),
        in_specs=[pl.BlockSpec((tm, kdim), lambda i: (i, 0)), _const_spec(w.shape)],
        out_specs=[o_spec] * (2 * n_out),
        out_shape=[jax.ShapeDtypeStruct((m, width), F32)] * n_out
        + [jax.ShapeDtypeStruct((m, width), BF16)] * n_out,
        compiler_params=_cparams(1),
        name="mm_split",
    )(x, w)


def _mm_kernel(x_ref, w_ref, o_ref):
    o_ref[...] = _dot(x_ref[...].astype(BF16), w_ref[...]).astype(o_ref.dtype)


def _mm(x, w, out_dtype, tm):
    m, kdim = x.shape
    n = w.shape[1]
    return pl.pallas_call(
        _mm_kernel,
        grid=(m // tm,),
        in_specs=[pl.BlockSpec((tm, kdim), lambda i: (i, 0)), _const_spec(w.shape)],
        out_specs=pl.BlockSpec((tm, n), lambda i: (i, 0)),
        out_shape=jax.ShapeDtypeStruct((m, n), out_dtype),
        compiler_params=_cparams(1),
        name="mm",
    )(x, w)


def _proj_in_kernel(x_ref, w_ref, q_ref, k_ref, v_ref, kb_ref, vb_ref, u_ref, *, sbw):
    xb = x_ref[...].astype(BF16)
    q = _dot(xb, w_ref[:, 0:sbw])
    q_ref[...] = (q * SB_SCALE).astype(BF16)
    k = _dot(xb, w_ref[:, sbw:2 * sbw])
    k_ref[...] = k
    kb_ref[...] = k.astype(BF16)
    v = _dot(xb, w_ref[:, 2 * sbw:3 * sbw])
    v_ref[...] = v
    vb_ref[...] = v.astype(BF16)
    u_ref[...] = _dot(xb, w_ref[:, 3 * sbw:])


def _proj_in(x, w, tm):
    m, d = x.shape
    sbw = d // 2
    pw = w.shape[1] - 3 * sbw
    row = lambda width: pl.BlockSpec((tm, width), lambda i: (i, 0))
    return pl.pallas_call(
        functools.partial(_proj_in_kernel, sbw=sbw),
        grid=(m // tm,),
        in_specs=[row(d), _const_spec(w.shape)],
        out_specs=[row(sbw), row(sbw), row(sbw), row(sbw), row(sbw), row(pw)],
        out_shape=[jax.ShapeDtypeStruct((m, sbw), BF16),
                   jax.ShapeDtypeStruct((m, sbw), F32),
                   jax.ShapeDtypeStruct((m, sbw), F32),
                   jax.ShapeDtypeStruct((m, sbw), BF16),
                   jax.ShapeDtypeStruct((m, sbw), BF16),
                   jax.ShapeDtypeStruct((m, pw), F32)],
        compiler_params=_cparams(1),
        name="proj_in",
    )(x, w)


def _proj_in_t_kernel(x_ref, wq_ref, wkt_ref, wvt_ref, wu_ref, q_ref, kt_ref, ktb_ref, vt_ref, vtb_ref, u_ref):
    xb = x_ref[0].astype(BF16)
    q_ref[0] = (_dot(xb, wq_ref[...]) * SB_SCALE).astype(BF16)
    kt = _dot_nt(wkt_ref[...], xb)
    kt_ref[0] = kt
    ktb_ref[0] = kt.astype(BF16)
    vt = _dot_nt(wvt_ref[...], xb)
    vt_ref[0] = vt
    vtb_ref[0] = vt.astype(BF16)
    u_ref[0] = _dot(xb, wu_ref[...])


def _proj_in_t(x, wq, wkt, wvt, wu, tm):
    b, t, d = x.shape
    sbw = wq.shape[1]
    pw = wu.shape[1]
    rows = lambda width: pl.BlockSpec((1, tm, width), lambda i, j: (i, j, 0))
    cols = pl.BlockSpec((1, sbw, tm), lambda i, j: (i, 0, j))
    return pl.pallas_call(
        _proj_in_t_kernel,
        grid=(b, t // tm),
        in_specs=[rows(d)] + [_const_spec(w.shape) for w in (wq, wkt, wvt, wu)],
        out_specs=[rows(sbw), cols, cols, cols, cols, rows(pw)],
        out_shape=[jax.ShapeDtypeStruct((b, t, sbw), BF16),
                   jax.ShapeDtypeStruct((b, sbw, t), F32),
                   jax.ShapeDtypeStruct((b, sbw, t), BF16),
                   jax.ShapeDtypeStruct((b, sbw, t), F32),
                   jax.ShapeDtypeStruct((b, sbw, t), BF16),
                   jax.ShapeDtypeStruct((b, t, pw), F32)],
        compiler_params=_cparams(2),
        name="proj_in_t",
    )(x, wq, wkt, wvt, wu)


def _pool_kernel(u_ref, buf_ref, w_ref, s_ref, o_ref, np_ref, halo_ref, *, tp, t_valid, start_pos):
    t = pl.program_id(1)

    @pl.when(t == 0)
    def _():
        halo_ref[...] = buf_ref[0]

    u = u_ref[0]
    ext = jnp.concatenate([halo_ref[...], u], axis=0)
    pos = start_pos + t * tp + lax.broadcasted_iota(jnp.int32, (tp, 1), 0)
    group = u.shape[1] // len(POOL_WINDOWS)
    for g, win in enumerate(POOL_WINDOWS):
        c0 = g * group
        s = ext[:, c0:c0 + group]
        span = 1
        while span < win:
            s = s + pltpu.roll(s, span, 0)
            span *= 2
        wsum = s[POOL_HALO:POOL_HALO + tp]
        inv_cnt = 1.0 / jnp.minimum(win, pos + 1).astype(F32)
        d = (wsum * inv_cnt - u[:, c0:c0 + group]).astype(BF16)
        y = _dot(d, w_ref[g]) * s_ref[:, c0:c0 + group]
        o_ref[0, :, c0:c0 + group] = y.astype(o_ref.dtype)
    tail = ext[t_valid:t_valid + POOL_HALO]
    halo_ref[...] = tail
    np_ref[0] = tail


def _pool(u, buf, w, scale, tp, t_valid, start_pos):
    b, t, width = u.shape
    return pl.pallas_call(
        functools.partial(_pool_kernel, tp=tp, t_valid=t_valid, start_pos=start_pos),
        grid=(b, t // tp),
        in_specs=[pl.BlockSpec((1, tp, width), lambda i, j: (i, j, 0)),
                  pl.BlockSpec((1, POOL_HALO, width), lambda i, j: (i, 0, 0)),
                  _const_spec(w.shape), _const_spec(scale.shape)],
        out_specs=[pl.BlockSpec((1, tp, width), lambda i, j: (i, j, 0)),
                   pl.BlockSpec((1, POOL_HALO, width), lambda i, j: (i, 0, 0))],
        out_shape=[jax.ShapeDtypeStruct((b, t, width), BF16),
                   jax.ShapeDtypeStruct((b, POOL_HALO, width), F32)],
        scratch_shapes=[pltpu.VMEM((POOL_HALO, width), F32)],
        compiler_params=_cparams(2),
        name="pool",
    )(u, buf, w, scale)


def _softplus(z, mask):
    neg_abs = lax.bitcast_convert_type(
        lax.bitcast_convert_type(z, jnp.uint32) | jnp.uint32(0x80000000), F32)
    sp = jnp.maximum(z, 0.0) + jnp.log(1.0 + jnp.exp(neg_abs))
    if mask is not None:
        sp = jnp.where(mask, sp, 0.0)
    return sp


def _suffix(sp, tri, carry):
    n = sp.shape[1]
    last = lax.broadcasted_iota(jnp.int32, (1, LANES), 1) == LANES - 1
    blocks = []
    tail = None
    for b in reversed(range(n // CUM_BLOCK)):
        spb = sp[:, b * CUM_BLOCK:(b + 1) * CUM_BLOCK]
        if tail is None:
            spb = jnp.concatenate([spb[:, :CUM_BLOCK - LANES],
                                   spb[:, CUM_BLOCK - LANES:] + jnp.where(last, carry, 0.0)], axis=1)
        hi = spb.astype(BF16)
        lo = (spb - hi.astype(F32)).astype(BF16)
        s = _dot(hi, tri) + _dot(lo, tri)
        if tail is not None:
            s = s + tail
        tail = s[:, 0:1]
        blocks.append(s)
    full = blocks[0] if len(blocks) == 1 else jnp.concatenate(blocks[::-1], axis=1)
    return full, tail


def _weights(z, s, mask):
    w = jnp.exp(z - s)
    if mask is not None:
        w = jnp.where(mask, w, 0.0)
    return w.astype(BF16)


def _suffix_matrix(n):
    r = lax.broadcasted_iota(jnp.int32, (n, n), 0)
    c = lax.broadcasted_iota(jnp.int32, (n, n), 1)
    return (r >= c).astype(BF16)


def _sb_prompt_kernel(bias_ref, q_ref, kt_ref, vt_ref, o_ref, *, tq):
    hp = pl.program_id(1)
    i = pl.program_id(2)
    lane = lax.broadcasted_iota(jnp.int32, (1, LANES), 1)
    first = lane < SB_HEAD_DIM
    top = lax.broadcasted_iota(jnp.int32, (LANES, 1), 0) < SB_HEAD_DIM
    q2 = q_ref[0]
    qz = jnp.zeros_like(q2)
    qm = (jnp.where(first, q2, qz), jnp.where(first, qz, q2))
    bias = (bias_ref[2 * hp], bias_ref[2 * hp + 1])
    tri = _suffix_matrix(CUM_BLOCK)
    r = lax.broadcasted_iota(jnp.int32, (tq, tq), 0)
    c = lax.broadcasted_iota(jnp.int32, (tq, tq), 1)
    causal = c < r

    def tile(j, carry, mask):
        c0, c1, acc = carry
        start = pl.multiple_of(j * tq, tq)
        kt2 = kt_ref[0, :, pl.ds(start, tq)]
        vt2 = vt_ref[0, :, pl.ds(start, tq)]
        vz = jnp.zeros_like(vt2)
        vm = (jnp.where(top, vt2, vz), jnp.where(top, vz, vt2))
        cs = (c0, c1)
        zs = [_dot(qm[h], kt2) + bias[h] for h in range(2)]
        sps = [_softplus(z, mask) for z in zs]
        sfx = [_suffix(sps[h], tri, cs[h]) for h in range(2)]
        ws = [_weights(zs[h], sfx[h][0], mask) for h in range(2)]
        for h in range(2):
            acc = acc + _dot_nt(ws[h], vm[h])
        return sfx[0][1], sfx[1][1], acc

    zero_c = jnp.zeros((tq, 1), F32)
    carry = tile(i, (zero_c, zero_c, jnp.zeros((tq, LANES), F32)), causal)
    carry = lax.fori_loop(0, i, lambda n, cr: tile(i - 1 - n, cr, None), carry)
    o_ref[0] = carry[2].astype(o_ref.dtype)


def _sb_prompt(q, kt, vt, bias, tq):
    b, t, width = q.shape
    return pl.pallas_call(
        functools.partial(_sb_prompt_kernel, tq=tq),
        grid=(b, width // LANES, t // tq),
        in_specs=[pl.BlockSpec(memory_space=pltpu.SMEM),
                  pl.BlockSpec((1, tq, LANES), lambda bi, hp, i: (bi, i, hp)),
                  pl.BlockSpec((1, LANES, t), lambda bi, hp, i: (bi, hp, 0)),
                  pl.BlockSpec((1, LANES, t), lambda bi, hp, i: (bi, hp, 0))],
        out_specs=pl.BlockSpec((1, tq, LANES), lambda bi, hp, i: (bi, i, hp)),
        out_shape=jax.ShapeDtypeStruct((b, t, width), BF16),
        compiler_params=_cparams(3),
        name="sb_prompt",
    )(bias, q, kt, vt)


def _sb_sample_kernel(pt_ref, bias_ref, q_ref, knt_ref, vnt_ref, *rest, n_pg, n_heads, t_new):
    kt_refs = rest[:n_pg]
    vt_refs = rest[n_pg:2 * n_pg]
    o_ref = rest[2 * n_pg]
    acc_ref, c_ref = rest[2 * n_pg + 1:]
    g = pl.program_id(1)
    rows = t_new * n_heads
    width = q_ref.shape[2]
    page = knt_ref.shape[2]
    row_head = lax.broadcasted_iota(jnp.int32, (rows, 1), 0) % n_heads
    lane_head = lax.broadcasted_iota(jnp.int32, (1, width), 1) // SB_HEAD_DIM
    own = row_head == lane_head
    q2 = q_ref[0]
    qm = jnp.where(own, q2, jnp.zeros_like(q2))
    bias = jnp.zeros((rows, 1), F32)
    for h in range(n_heads):
        bias = jnp.where(row_head == h, bias_ref[h], bias)
    tri = _suffix_matrix(CUM_BLOCK)

    def attend(kts, vts, mask):
        z = jnp.concatenate([_dot(qm, kt) for kt in kts], axis=1) + bias
        if z.shape[1] % CUM_BLOCK:
            pad = CUM_BLOCK - z.shape[1] % CUM_BLOCK
            z = jnp.concatenate([z, jnp.zeros((rows, pad), F32)], axis=1)
            mask = jnp.concatenate([mask, jnp.zeros((rows, pad), jnp.bool_)], axis=1)
        s, c_new = _suffix(_softplus(z, mask), tri, c_ref[...])
        w = _weights(z, s, mask)
        c_ref[...] = c_new
        acc = acc_ref[...]
        for p, vt in enumerate(vts):
            wt = jnp.transpose(w[:, p * page:(p + 1) * page].astype(F32)).astype(BF16)
            acc = acc + _dot(vt, wt)
        acc_ref[...] = acc

    @pl.when(g == 0)
    def _():
        acc_ref[...] = jnp.zeros_like(acc_ref)
        c_ref[...] = jnp.zeros_like(c_ref)
        s_idx = lax.broadcasted_iota(jnp.int32, (rows, page), 1)
        t_idx = lax.broadcasted_iota(jnp.int32, (rows, page), 0) // n_heads
        attend([knt_ref[0]], [vnt_ref[0]], s_idx < t_idx)

    attend([ref[0].astype(BF16) for ref in kt_refs], [ref[0].astype(BF16) for ref in vt_refs], None)

    @pl.when(g == pl.num_programs(1) - 1)
    def _():
        o = jnp.where(own, jnp.transpose(acc_ref[...]), 0.0).reshape(t_new, n_heads, width)
        o_ref[0] = jnp.sum(o, axis=1)


def _sb_sample(page_table, bias, q_rep, k_new_t, v_new_t, cache_kt, cache_vt, layer_base, n_pg, t_new):
    b, rows, width = q_rep.shape
    page = cache_kt.shape[2]
    n_pages = page_table.shape[0] // b
    steps = n_pages // n_pg
    n_heads = rows // t_new

    def page_spec(p):
        def imap(bi, g, pt):
            return (layer_base + pt[bi * n_pages + (steps - 1 - g) * n_pg + p], 0, 0)
        return pl.BlockSpec((1, width, page), imap)

    per_b = lambda shape: pl.BlockSpec(shape, lambda bi, g, pt: (bi, 0, 0))
    grid_spec = pltpu.PrefetchScalarGridSpec(
        num_scalar_prefetch=1,
        grid=(b, steps),
        in_specs=[pl.BlockSpec(memory_space=pltpu.SMEM),
                  per_b((1, rows, width)), per_b((1, width, page)), per_b((1, width, page))]
        + [page_spec(p) for p in range(n_pg)] * 2,
        out_specs=per_b((1, t_new, width)),
        scratch_shapes=[pltpu.VMEM((width, rows), F32), pltpu.VMEM((rows, 1), F32)],
    )
    return pl.pallas_call(
        functools.partial(_sb_sample_kernel, n_pg=n_pg, n_heads=n_heads, t_new=t_new),
        grid_spec=grid_spec,
        out_shape=jax.ShapeDtypeStruct((b, t_new, width), F32),
        compiler_params=_cparams(2),
        name="sb_sample",
    )(page_table, bias, q_rep, k_new_t, v_new_t, *([cache_kt] * n_pg), *([cache_vt] * n_pg))


def _proj_ln_kernel(*refs, n_in, alpha):
    a_refs = refs[:n_in]
    w_refs = refs[n_in:2 * n_in]
    x_ref, g_ref, b_ref, o_ref = refs[2 * n_in:]
    y = alpha * x_ref[...]
    for a_ref, w_ref in zip(a_refs, w_refs):
        y = y + _dot(a_ref[...].astype(BF16), w_ref[...])
    o_ref[...] = _layer_norm(y, g_ref[...], b_ref[...])


def _proj_ln(a_list, w_list, x, g, b, alpha, tm):
    m, d = x.shape
    row = lambda width: pl.BlockSpec((tm, width), lambda i: (i, 0))
    return pl.pallas_call(
        functools.partial(_proj_ln_kernel, n_in=len(a_list), alpha=alpha),
        grid=(m // tm,),
        in_specs=[row(a.shape[1]) for a in a_list] + [_const_spec(w.shape) for w in w_list]
        + [row(d), _const_spec(g.shape), _const_spec(b.shape)],
        out_specs=row(d),
        out_shape=jax.ShapeDtypeStruct((m, d), F32),
        compiler_params=_cparams(1),
        name="proj_ln",
    )(*a_list, *w_list, x, g, b)


def _mem_core_kernel(q_ref, k_ref, v_ref, o_ref, *, scale):
    hd = q_ref.shape[2] // N_MEM_HEADS
    for h in range(N_MEM_HEADS):
        cols = slice(h * hd, (h + 1) * hd)
        s = _dot_nt(q_ref[0, :, cols].astype(BF16), k_ref[0, :, cols].astype(BF16)) * scale
        p = jnp.exp(s - jnp.max(s, axis=-1, keepdims=True))
        p = p * (1.0 / jnp.sum(p, axis=-1, keepdims=True))
        o_ref[0, :, cols] = _dot(p.astype(BF16), v_ref[0, :, cols].astype(BF16)).astype(o_ref.dtype)


def _mem_core(q, mem_k, mem_v, tm, base=0):
    b, t, d = q.shape
    n_mem = mem_k.shape[1]
    scale = 1.0 / math.sqrt(d // N_MEM_HEADS)
    return pl.pallas_call(
        functools.partial(_mem_core_kernel, scale=scale),
        grid=(b, t // tm),
        in_specs=[pl.BlockSpec((1, tm, d), lambda i, j: (i, j, 0)),
                  pl.BlockSpec((1, n_mem, d), lambda i, j: (base + i, 0, 0)),
                  pl.BlockSpec((1, n_mem, d), lambda i, j: (base + i, 0, 0))],
        out_specs=pl.BlockSpec((1, tm, d), lambda i, j: (i, j, 0)),
        out_shape=jax.ShapeDtypeStruct((b, t, d), q.dtype),
        compiler_params=_cparams(2),
        name="mem_core",
    )(q, mem_k, mem_v)


def _gelu_tanh(x):
    return 0.5 * x * (1.0 + jnp.tanh(math.sqrt(2.0 / math.pi) * (x + 0.044715 * (x * x * x))))


def _ffn_kernel(*refs, seq, d_ff, ck, alpha):
    if seq is None:
        x_ref, wu_ref, cw_ref, cb_ref, wd_ref, g_ref, b_ref, st_ref, o_ref, ns_ref, prev_ref = refs
        x = x_ref[0]
    else:
        x_ref, wu_ref, cw_ref, cb_ref, wd_ref, g_ref, b_ref, p1_ref, p2_ref, o_ref, h_ref = refs
        x = x_ref[...]
    tm = x.shape[0]
    n_prev = CONV_WIDTH - 1
    row = lax.broadcasted_iota(jnp.int32, (tm, 1), 0)
    if seq is None:
        @pl.when(pl.program_id(1) == 0)
        def _():
            prev_ref[...] = st_ref[0]
    else:
        row = row % seq

    xb = x.astype(BF16)
    y = alpha * x
    n_chunks = d_ff // ck

    def up(c):
        out = []
        for part in range(2):
            cols = slice(part * d_ff + c * ck, part * d_ff + (c + 1) * ck)
            h = _dot(xb, wu_ref[:, cols])
            if seq is None:
                p1 = prev_ref[1:2, cols]
                p2 = jnp.where(row == 1, p1, prev_ref[0:1, cols])
                prev_ref[:, cols] = h[tm - n_prev:]
                ns_ref[0, :, cols] = h[tm - n_prev:]
            else:
                p1 = p1_ref[:, cols]
                p2 = p2_ref[:, cols]
                h_ref[:, cols] = h
            out.append((h, p1, p2, cols))
        return out

    nxt = up(0)
    for c in range(n_chunks):
        cur = nxt
        if c + 1 < n_chunks:
            nxt = up(c + 1)
        conv = []
        for h, p1, p2, cols in cur:
            h1 = jnp.where(row >= 1, pltpu.roll(h, 1, 0), p1)
            h2 = jnp.where(row >= 2, pltpu.roll(h, 2, 0), p2)
            conv.append(cb_ref[:, cols] + cw_ref[0:1, cols] * h2 + cw_ref[1:2, cols] * h1
                        + cw_ref[2:3, cols] * h)
        act = (_gelu_tanh(conv[0]) * conv[1]).astype(BF16)
        y = y + _dot(act, wd_ref[c * ck:(c + 1) * ck, :])
    out = _layer_norm(y, g_ref[...], b_ref[...])
    if seq is None:
        o_ref[0] = out
    else:
        o_ref[...] = out


def _ffn_prompt(x, w_up, conv_w, conv_b, w_down, g, b, state, alpha, tm, ck):
    bsz, t, d = x.shape
    d_ff = w_down.shape[0]
    consts = [w_up, conv_w, conv_b, w_down, g, b]
    return pl.pallas_call(
        functools.partial(_ffn_kernel, seq=None, d_ff=d_ff, ck=ck, alpha=alpha),
        grid=(bsz, t // tm),
        in_specs=[pl.BlockSpec((1, tm, d), lambda i, j: (i, j, 0))]
        + [_const_spec(a.shape) for a in consts]
        + [pl.BlockSpec((1,) + state.shape[1:], lambda i, j: (i, 0, 0))],
        out_specs=[pl.BlockSpec((1, tm, d), lambda i, j: (i, j, 0)),
                   pl.BlockSpec((1,) + state.shape[1:], lambda i, j: (i, 0, 0))],
        out_shape=[jax.ShapeDtypeStruct(x.shape, F32), jax.ShapeDtypeStruct(state.shape, F32)],
        scratch_shapes=[pltpu.VMEM(state.shape[1:], F32)],
        compiler_params=_cparams(2),
        name="ffn_prompt",
    )(x, *consts, state)


def _ffn_sample(x, w_up, conv_w, conv_b, w_down, g, b, prev1, prev2, seq, alpha, ck):
    m, d = x.shape
    d_ff = w_down.shape[0]
    args = [x, w_up, conv_w, conv_b, w_down, g, b, prev1, prev2]
    return pl.pallas_call(
        functools.partial(_ffn_kernel, seq=seq, d_ff=d_ff, ck=ck, alpha=alpha),
        grid=(1,),
        in_specs=[_const_spec(a.shape) for a in args],
        out_specs=[_const_spec((m, d)), _const_spec((m, 2 * d_ff))],
        out_shape=[jax.ShapeDtypeStruct((m, d), F32), jax.ShapeDtypeStruct((m, 2 * d_ff), F32)],
        compiler_params=_cparams(1),
        name="ffn_sample",
    )(*args)


def kernel(x_prompt, x_sample, mem_prompt, cache_k, cache_v, page_table, cache_mem_k, cache_mem_v, state_pool, state_conv, w_in, sb_bias, pool_w, pool_scale, w_out, ln_mix_g, ln_mix_b, w_mq, w_mk, w_mv, w_mo, ln_mem_g, ln_mem_b, w_up, conv_w, conv_b, w_down, ln_ffn_g, ln_ffn_b):
    depth = w_in.shape[0]
    bsz, seq, d = x_prompt.shape
    dbsz, dseq, _ = x_sample.shape
    sbw = d // 2
    n_heads = sbw // SB_HEAD_DIM
    n_mem = mem_prompt.shape[1]
    d_ff = w_down.shape[1]
    n_phys, page = cache_k.shape[1], cache_k.shape[2]
    alpha = (2 * depth) ** 0.25
    m_p = bsz * seq
    m_s = dbsz * dseq

    w_in_b, pool_w_b, w_out_b = w_in.astype(BF16), pool_w.astype(BF16), w_out.astype(BF16)
    w_mq_b, w_mo_b, w_up_b, w_down_b = (w_mq.astype(BF16), w_mo.astype(BF16), w_up.astype(BF16),
                                        w_down.astype(BF16))
    row2 = lambda a, l: a[l][None, :]

    w_mkv = jnp.concatenate([w_mk[l] for l in range(depth)] + [w_mv[l] for l in range(depth)],
                            axis=1).astype(BF16)
    mem_out = _mm_split(mem_prompt.reshape(bsz * n_mem, d), w_mkv, 2 * depth, tm=512)
    mem_f32, mem_b16 = mem_out[:2 * depth], mem_out[2 * depth:]

    def mixer_tail(l, x2d, o_sb, o_pool, mem_attend, tm, q_dtype):
        x1 = _proj_ln([o_sb, o_pool], [w_out_b[l, :sbw], w_out_b[l, sbw:]], x2d,
                      row2(ln_mix_g, l), row2(ln_mix_b, l), alpha, tm)
        qm = _mm(x1, w_mq_b[l], q_dtype, tm)
        om = mem_attend(qm)
        return _proj_ln([om], [w_mo_b[l]], x1, row2(ln_mem_g, l), row2(ln_mem_b, l), alpha, tm)

    xp = x_prompt.reshape(m_p, d)
    kp, vp, pp, cp = [], [], [], []
    pool0 = jnp.zeros((bsz, POOL_HALO, d - sbw), F32)
    conv0 = jnp.zeros((bsz, CONV_WIDTH - 1, 2 * d_ff), F32)
    for l in range(depth):
        wl = w_in_b[l]
        q, k, kb, v, vb, u = _proj_in_t(xp.reshape(bsz, seq, d), wl[:, :sbw], wl[:, sbw:2 * sbw].T,
                                        wl[:, 2 * sbw:3 * sbw].T, wl[:, 3 * sbw:], tm=512)
        o_sb = _sb_prompt(q, kb, vb, sb_bias[l], tq=512)
        o_pool, npool = _pool(u, pool0, pool_w_b[l], row2(pool_scale, l), tp=512, t_valid=512, start_pos=0)
        mem_k = mem_b16[l].reshape(bsz, n_mem, d)
        mem_v = mem_b16[depth + l].reshape(bsz, n_mem, d)
        x2 = mixer_tail(l, xp, o_sb.reshape(m_p, sbw), o_pool.reshape(m_p, d - sbw),
                        lambda qm: _mem_core(qm.reshape(bsz, seq, d), mem_k, mem_v, tm=512).reshape(m_p, d),
                        tm=512, q_dtype=BF16)
        x3, nconv = _ffn_prompt(x2.reshape(bsz, seq, d), w_up_b[l], conv_w[l], row2(conv_b, l), w_down_b[l],
                                row2(ln_ffn_g, l), row2(ln_ffn_b, l), conv0, alpha, tm=512, ck=256)
        xp = x3.reshape(m_p, d)
        kp.append(k)
        vp.append(v)
        pp.append(npool[:, 1:])
        cp.append(nconv)
    heads_last = lambda ts: jnp.stack(ts).reshape(depth, bsz, n_heads, SB_HEAD_DIM, seq).transpose(0, 1, 4, 2, 3)
    mkp = [mem_f32[l].reshape(bsz, n_mem, N_MEM_HEADS, d // N_MEM_HEADS) for l in range(depth)]
    mvp = [mem_f32[depth + l].reshape(bsz, n_mem, N_MEM_HEADS, d // N_MEM_HEADS) for l in range(depth)]

    xs = x_sample.reshape(m_s, d)
    past = page_table.shape[1] * page
    pt_flat = page_table.reshape(-1)
    pages_t = lambda c: c.transpose(0, 1, 3, 4, 2).reshape(depth * n_phys, sbw, page)
    ckt, cvt = pages_t(cache_k), pages_t(cache_v)
    cmk = cache_mem_k.reshape(depth * dbsz, n_mem, d)
    cmv = cache_mem_v.reshape(depth * dbsz, n_mem, d)
    ks_, vs_, ps_, cs_ = [], [], [], []
    seq_pad = 8
    for l in range(depth):
        q, k, v, kb, vb, u = _proj_in(xs, w_in_b[l], tm=m_s)
        q_rep = jnp.repeat(q.reshape(dbsz, dseq, sbw), n_heads, axis=1)
        new_t = lambda a: jnp.pad(a.reshape(dbsz, dseq, sbw).transpose(0, 2, 1),
                                  ((0, 0), (0, 0), (0, page - dseq)))
        o_sb = _sb_sample(pt_flat, sb_bias[l], q_rep, new_t(kb), new_t(vb), ckt, cvt,
                          l * n_phys, n_pg=16, t_new=dseq)
        u_pad = jnp.pad(u.reshape(dbsz, dseq, d - sbw), ((0, 0), (0, seq_pad - dseq), (0, 0)))
        buf = jnp.pad(state_pool[l], ((0, 0), (1, 0), (0, 0)))
        o_pool, npool = _pool(u_pad, buf, pool_w_b[l], row2(pool_scale, l), tp=seq_pad, t_valid=dseq,
                              start_pos=past)
        o_pool = o_pool[:, :dseq].reshape(m_s, d - sbw)
        def mem_sample(qm, l=l):
            q_pad = jnp.pad(qm.reshape(dbsz, dseq, d), ((0, 0), (0, seq_pad - dseq), (0, 0)))
            om = _mem_core(q_pad, cmk, cmv, tm=seq_pad, base=l * dbsz)
            return om[:, :dseq].reshape(m_s, d)

        x2 = mixer_tail(l, xs, o_sb.reshape(m_s, sbw), o_pool, mem_sample, tm=m_s, q_dtype=F32)
        st = state_conv[l]
        zrow = jnp.zeros_like(st[:, :1])
        prev1 = jnp.concatenate([st[:, 1:2]] + [zrow] * (dseq - 1), axis=1).reshape(m_s, 2 * d_ff)
        prev2 = jnp.concatenate([st[:, 0:1], st[:, 1:2]] + [zrow] * (dseq - 2), axis=1).reshape(m_s, 2 * d_ff)
        xs, h = _ffn_sample(x2, w_up_b[l], conv_w[l], row2(conv_b, l), w_down_b[l],
                            row2(ln_ffn_g, l), row2(ln_ffn_b, l), prev1, prev2, dseq, alpha, ck=256)
        ks_.append(k.reshape(dbsz, dseq, n_heads, SB_HEAD_DIM))
        vs_.append(v.reshape(dbsz, dseq, n_heads, SB_HEAD_DIM))
        ps_.append(npool[:, 1:])
        cs_.append(h.reshape(dbsz, dseq, 2 * d_ff)[:, dseq - (CONV_WIDTH - 1):])

    return (xp.reshape(bsz, seq, d), xs.reshape(dbsz, dseq, d),
            heads_last(kp), heads_last(vp), jnp.stack(mkp), jnp.stack(mvp), jnp.stack(pp), jnp.stack(cp),
            jnp.stack(ks_), jnp.stack(vs_), jnp.stack(ps_), jnp.stack(cs_))
```

```python
import functools
import math

import jax
import jax.numpy as jnp
from jax import lax
from jax.experimental import pallas as pl
from jax.experimental.pallas import tpu as pltpu

F32 = jnp.float32
BF16 = jnp.bfloat16

SB_HEAD_DIM = 64
POOL_WINDOWS = (2, 4, 8, 16)
POOL_BUF = max(POOL_WINDOWS) - 1
POOL_HALO = POOL_BUF + 1
N_MEM_HEADS = 4
CONV_WIDTH = 3
LN_EPS = 1e-5
SB_SCALE = 1.0 / math.sqrt(SB_HEAD_DIM)
LANES = 128
CUM_BLOCK = 256
VMEM_LIMIT = 48 * 1024 * 1024


def _cparams(n_grid_dims):
    return pltpu.CompilerParams(
        dimension_semantics=("arbitrary",) * n_grid_dims,
        vmem_limit_bytes=VMEM_LIMIT)


def _const_spec(shape):
    nd = len(shape)
    return pl.BlockSpec(shape, lambda *_: (0,) * nd, pipeline_mode=pl.Buffered(1))


def _layer_norm(y, g, b):
    mu = jnp.mean(y, axis=-1, keepdims=True)
    yc = y - mu
    var = jnp.mean(yc * yc, axis=-1, keepdims=True)
    return yc * lax.rsqrt(var + LN_EPS) * g + b


def _dot(a, b):
    return jnp.dot(a, b, preferred_element_type=F32)


def _dot_nt(a, b):
    return lax.dot_general(a, b, (((1,), (1,)), ((), ())), preferred_element_type=F32)


def _mm_split_kernel(x_ref, w_ref, *o_refs, n_out, width):
    xb = x_ref[...].astype(BF16)
    for i in range(n_out):
        y = _dot(xb, w_ref[:, i * width:(i + 1) * width])
        o_refs[i][...] = y
        o_refs[n_out + i][...] = y.astype(BF16)


def _mm_split(x, w, n_out, tm):
    m, kdim = x.shape
    width = w.shape[1] // n_out
    o_spec = pl.BlockSpec((tm, width), lambda i: (i, 0))
    return pl.pallas_call(
        functools.partial(_mm_split_kernel, n_out=n_out, width=width),
        grid=(m // tm,),
        in_specs=[pl.BlockSpec((tm, kdim), lambda i: (i, 0)), _const_spec(w.shape)],
        out_specs=[o_spec] * (2 * n_out),
        out_shape=[jax.ShapeDtypeStruct((m, width), F32)] * n_out
        + [jax.ShapeDtypeStruct((m, width), BF16)] * n_out,
        compiler_params=_cparams(1),
        name="mm_split",
    )(x, w)


def _mm_kernel(x_ref, w_ref, o_ref):
    o_ref[...] = _dot(x_ref[...].astype(BF16), w_ref[...]).astype(o_ref.dtype)


def _mm(x, w, out_dtype, tm):
    m, kdim = x.shape
    n = w.shape[1]
    return pl.pallas_call(
        _mm_kernel,
        grid=(m // tm,),
        in_specs=[pl.BlockSpec((tm, kdim), lambda i: (i, 0)), _const_spec(w.shape)],
        out_specs=pl.BlockSpec((tm, n), lambda i: (i, 0)),
        out_shape=jax.ShapeDtypeStruct((m, n), out_dtype),
        compiler_params=_cparams(1),
        name="mm",
    )(x, w)


def _proj_in_kernel(x_ref, w_ref, q_ref, k_ref, v_ref, kb_ref, vb_ref, u_ref, *, sbw):
    xb = x_ref[...].astype(BF16)
    q = _dot(xb, w_ref[:, 0:sbw])
    q_ref[...] = (q * SB_SCALE).astype(BF16)
    k = _dot(xb, w_ref[:, sbw:2 * sbw])
    k_ref[...] = k
    kb_ref[...] = k.astype(BF16)
    v = _dot(xb, w_ref[:, 2 * sbw:3 * sbw])
    v_ref[...] = v
    vb_ref[...] = v.astype(BF16)
    u_ref[...] = _dot(xb, w_ref[:, 3 * sbw:])


def _proj_in(x, w, tm):
    m, d = x.shape
    sbw = d // 2
    pw = w.shape[1] - 3 * sbw
    row = lambda width: pl.BlockSpec((tm, width), lambda i: (i, 0))
    return pl.pallas_call(
        functools.partial(_proj_in_kernel, sbw=sbw),
        grid=(m // tm,),
        in_specs=[row(d), _const_spec(w.shape)],
        out_specs=[row(sbw), row(sbw), row(sbw), row(sbw), row(sbw), row(pw)],
        out_shape=[jax.ShapeDtypeStruct((m, sbw), BF16),
                   jax.ShapeDtypeStruct((m, sbw), F32),
                   jax.ShapeDtypeStruct((m, sbw), F32),
                   jax.ShapeDtypeStruct((m, sbw), BF16),
                   jax.ShapeDtypeStruct((m, sbw), BF16),
                   jax.ShapeDtypeStruct((m, pw), F32)],
        compiler_params=_cparams(1),
        name="proj_in",
    )(x, w)


def _proj_in_t_kernel(x_ref, wq_ref, wkt_ref, wvt_ref, wu_ref, buf_ref, pw_ref, ps_ref,
                      q_ref, kt_ref, ktb_ref, vt_ref, vtb_ref, op_ref, np_ref, halo_ref, *, tm):
    xb = x_ref[0].astype(BF16)
    q_ref[0] = (_dot(xb, wq_ref[...]) * SB_SCALE).astype(BF16)
    kt = _dot_nt(wkt_ref[...], xb)
    kt_ref[0] = kt
    ktb_ref[0] = kt.astype(BF16)
    vt = _dot_nt(wvt_ref[...], xb)
    vt_ref[0] = vt
    vtb_ref[0] = vt.astype(BF16)
    _pool_tile(_dot(xb, wu_ref[...]), buf_ref, pw_ref, ps_ref, op_ref, np_ref, halo_ref,
               tp=tm, t_valid=tm, start_pos=0)


def _proj_in_t(x, wq, wkt, wvt, wu, pool_buf, pool_w, pool_scale, tm):
    b, t, d = x.shape
    sbw = wq.shape[1]
    pw = wu.shape[1]
    rows = lambda width: pl.BlockSpec((1, tm, width), lambda i, j: (i, j, 0))
    cols = pl.BlockSpec((1, sbw, tm), lambda i, j: (i, 0, j))
    halo = pl.BlockSpec((1, POOL_HALO, pw), lambda i, j: (i, 0, 0))
    consts = (wq, wkt, wvt, wu)
    return pl.pallas_call(
        functools.partial(_proj_in_t_kernel, tm=tm),
        grid=(b, t // tm),
        in_specs=[rows(d)] + [_const_spec(w.shape) for w in consts]
        + [halo, _const_spec(pool_w.shape), _const_spec(pool_scale.shape)],
        out_specs=[rows(sbw), cols, cols, cols, cols, rows(pw), halo],
        out_shape=[jax.ShapeDtypeStruct((b, t, sbw), BF16),
                   jax.ShapeDtypeStruct((b, sbw, t), F32),
                   jax.ShapeDtypeStruct((b, sbw, t), BF16),
                   jax.ShapeDtypeStruct((b, sbw, t), F32),
                   jax.ShapeDtypeStruct((b, sbw, t), BF16),
                   jax.ShapeDtypeStruct((b, t, pw), BF16),
                   jax.ShapeDtypeStruct((b, POOL_HALO, pw), F32)],
        scratch_shapes=[pltpu.VMEM((POOL_HALO, pw), F32)],
        compiler_params=_cparams(2),
        name="proj_in_t",
    )(x, *consts, pool_buf, pool_w, pool_scale)


def _pool_kernel(u_ref, buf_ref, w_ref, s_ref, o_ref, np_ref, halo_ref, *, tp, t_valid, start_pos):
    _pool_tile(u_ref[0], buf_ref, w_ref, s_ref, o_ref, np_ref, halo_ref, tp=tp, t_valid=t_valid,
               start_pos=start_pos)


def _pool_tile(u, buf_ref, w_ref, s_ref, o_ref, np_ref, halo_ref, *, tp, t_valid, start_pos):
    t = pl.program_id(1)

    @pl.when(t == 0)
    def _():
        halo_ref[...] = buf_ref[0]

    ext = jnp.concatenate([halo_ref[...], u], axis=0)
    pos = start_pos + t * tp + lax.broadcasted_iota(jnp.int32, (tp, 1), 0)
    group = u.shape[1] // len(POOL_WINDOWS)
    for g, win in enumerate(POOL_WINDOWS):
        c0 = g * group
        s = ext[:, c0:c0 + group]
        span = 1
        while span < win:
            s = s + pltpu.roll(s, span, 0)
            span *= 2
        wsum = s[POOL_HALO:POOL_HALO + tp]
        inv_cnt = 1.0 / jnp.minimum(win, pos + 1).astype(F32)
        d = (wsum * inv_cnt - u[:, c0:c0 + group]).astype(BF16)
        y = _dot(d, w_ref[g]) * s_ref[:, c0:c0 + group]
        o_ref[0, :, c0:c0 + group] = y.astype(o_ref.dtype)
    tail = ext[t_valid:t_valid + POOL_HALO]
    halo_ref[...] = tail
    np_ref[0] = tail


def _pool(u, buf, w, scale, tp, t_valid, start_pos):
    b, t, width = u.shape
    return pl.pallas_call(
        functools.partial(_pool_kernel, tp=tp, t_valid=t_valid, start_pos=start_pos),
        grid=(b, t // tp),
        in_specs=[pl.BlockSpec((1, tp, width), lambda i, j: (i, j, 0)),
                  pl.BlockSpec((1, POOL_HALO, width), lambda i, j: (i, 0, 0)),
                  _const_spec(w.shape), _const_spec(scale.shape)],
        out_specs=[pl.BlockSpec((1, tp, width), lambda i, j: (i, j, 0)),
                   pl.BlockSpec((1, POOL_HALO, width), lambda i, j: (i, 0, 0))],
        out_shape=[jax.ShapeDtypeStruct((b, t, width), BF16),
                   jax.ShapeDtypeStruct((b, POOL_HALO, width), F32)],
        scratch_shapes=[pltpu.VMEM((POOL_HALO, width), F32)],
        compiler_params=_cparams(2),
        name="pool",
    )(u, buf, w, scale)


def _softplus(z, mask):
    neg_abs = lax.bitcast_convert_type(
        lax.bitcast_convert_type(z, jnp.uint32) | jnp.uint32(0x80000000), F32)
    sp = jnp.maximum(z, 0.0) + jnp.log(1.0 + jnp.exp(neg_abs))
    if mask is not None:
        sp = jnp.where(mask, sp, 0.0)
    return sp


def _suffix(sp, tri, carry, split):
    n = sp.shape[1]
    last = lax.broadcasted_iota(jnp.int32, (1, LANES), 1) == LANES - 1
    blocks = []
    tail = None
    for b in reversed(range(n // CUM_BLOCK)):
        spb = sp[:, b * CUM_BLOCK:(b + 1) * CUM_BLOCK]
        if tail is None:
            spb = jnp.concatenate([spb[:, :CUM_BLOCK - LANES],
                                   spb[:, CUM_BLOCK - LANES:] + jnp.where(last, carry, 0.0)], axis=1)
        hi = spb.astype(BF16)
        s = _dot(hi, tri)
        if split:
            s = s + _dot((spb - hi.astype(F32)).astype(BF16), tri)
        if tail is not None:
            s = s + tail
        tail = s[:, 0:1]
        blocks.append(s)
    full = blocks[0] if len(blocks) == 1 else jnp.concatenate(blocks[::-1], axis=1)
    return full, tail


def _weights(z, s, mask):
    w = jnp.exp(z - s)
    if mask is not None:
        w = jnp.where(mask, w, 0.0)
    return w.astype(BF16)


def _suffix_matrix(n):
    r = lax.broadcasted_iota(jnp.int32, (n, n), 0)
    c = lax.broadcasted_iota(jnp.int32, (n, n), 1)
    return (r >= c).astype(BF16)


def _sb_prompt_kernel(bias_ref, q_ref, kt_ref, vt_ref, o_ref, *, tq, tk):
    hp = pl.program_id(1)
    i = pl.program_id(2)
    lane = lax.broadcasted_iota(jnp.int32, (1, LANES), 1)
    first = lane < SB_HEAD_DIM
    top = lax.broadcasted_iota(jnp.int32, (LANES, 1), 0) < SB_HEAD_DIM
    q2 = q_ref[0]
    qz = jnp.zeros_like(q2)
    qm = (jnp.where(first, q2, qz), jnp.where(first, qz, q2))
    bias = (bias_ref[2 * hp], bias_ref[2 * hp + 1])
    tri = _suffix_matrix(CUM_BLOCK)
    r = lax.broadcasted_iota(jnp.int32, (tq, tq), 0)
    c = lax.broadcasted_iota(jnp.int32, (tq, tq), 1)
    causal = c < r

    def tile(start, width, carry, mask):
        c0, c1, acc = carry
        start = pl.multiple_of(start, tq)
        kt2 = kt_ref[0, :, pl.ds(start, width)]
        vt2 = vt_ref[0, :, pl.ds(start, width)]
        vz = jnp.zeros_like(vt2)
        vm = (jnp.where(top, vt2, vz), jnp.where(top, vz, vt2))
        cs = (c0, c1)
        zs = [_dot(qm[h], kt2) + bias[h] for h in range(2)]
        sps = [_softplus(z, mask) for z in zs]
        sfx = [_suffix(sps[h], tri, cs[h], split=False) for h in range(2)]
        ws = [_weights(zs[h], sfx[h][0], mask) for h in range(2)]
        for h in range(2):
            acc = acc + _dot_nt(ws[h], vm[h])
        return sfx[0][1], sfx[1][1], acc

    zero_c = jnp.zeros((tq, 1), F32)
    carry = tile(i * tq, tq, (zero_c, zero_c, jnp.zeros((tq, LANES), F32)), causal)
    ratio = tk // tq
    odd = i % ratio
    carry = lax.cond(odd == 1, lambda cr: tile((i - 1) * tq, tq, cr, None), lambda cr: cr, carry)
    base = (i - odd) * tq
    carry = lax.fori_loop(0, (i - odd) // ratio, lambda n, cr: tile(base - (n + 1) * tk, tk, cr, None), carry)
    o_ref[0] = carry[2].astype(o_ref.dtype)


def _sb_prompt(q, kt, vt, bias, tq, tk):
    b, t, width = q.shape
    assert tk == 2 * tq and t % tq == 0
    return pl.pallas_call(
        functools.partial(_sb_prompt_kernel, tq=tq, tk=tk),
        grid=(b, width // LANES, t // tq),
        in_specs=[pl.BlockSpec(memory_space=pltpu.SMEM),
                  pl.BlockSpec((1, tq, LANES), lambda bi, hp, i: (bi, i, hp)),
                  pl.BlockSpec((1, LANES, t), lambda bi, hp, i: (bi, hp, 0)),
                  pl.BlockSpec((1, LANES, t), lambda bi, hp, i: (bi, hp, 0))],
        out_specs=pl.BlockSpec((1, tq, LANES), lambda bi, hp, i: (bi, i, hp)),
        out_shape=jax.ShapeDtypeStruct((b, t, width), BF16),
        compiler_params=_cparams(3),
        name="sb_prompt",
    )(bias, q, kt, vt)


def _sb_sample_kernel(pt_ref, bias_ref, q_ref, knt_ref, vnt_ref, *rest, n_pg, n_heads, t_new):
    kt_refs = rest[:n_pg]
    vt_refs = rest[n_pg:2 * n_pg]
    o_ref = rest[2 * n_pg]
    acc_ref, c_ref = rest[2 * n_pg + 1:]
    g = pl.program_id(1)
    rows = t_new * n_heads
    width = q_ref.shape[2]
    page = knt_ref.shape[2]
    row_head = lax.broadcasted_iota(jnp.int32, (rows, 1), 0) % n_heads
    lane_head = lax.broadcasted_iota(jnp.int32, (1, width), 1) // SB_HEAD_DIM
    own = row_head == lane_head
    q2 = q_ref[0]
    qm = jnp.where(own, q2, jnp.zeros_like(q2))
    bias = jnp.zeros((rows, 1), F32)
    for h in range(n_heads):
        bias = jnp.where(row_head == h, bias_ref[h], bias)
    tri = _suffix_matrix(CUM_BLOCK)

    def attend(kts, vts, mask):
        z = jnp.concatenate([_dot(qm, kt) for kt in kts], axis=1) + bias
        if z.shape[1] % CUM_BLOCK:
            pad = CUM_BLOCK - z.shape[1] % CUM_BLOCK
            z = jnp.concatenate([z, jnp.zeros((rows, pad), F32)], axis=1)
            mask = jnp.concatenate([mask, jnp.zeros((rows, pad), jnp.bool_)], axis=1)
        s, c_new = _suffix(_softplus(z, mask), tri, c_ref[...], split=True)
        w = _weights(z, s, mask)
        c_ref[...] = c_new
        acc = acc_ref[...]
        for p, vt in enumerate(vts):
            wt = jnp.transpose(w[:, p * page:(p + 1) * page].astype(F32)).astype(BF16)
            acc = acc + _dot(vt, wt)
        acc_ref[...] = acc

    @pl.when(g == 0)
    def _():
        acc_ref[...] = jnp.zeros_like(acc_ref)
        c_ref[...] = jnp.zeros_like(c_ref)
        s_idx = lax.broadcasted_iota(jnp.int32, (rows, page), 1)
        t_idx = lax.broadcasted_iota(jnp.int32, (rows, page), 0) // n_heads
        attend([knt_ref[0]], [vnt_ref[0]], s_idx < t_idx)

    attend([ref[0].astype(BF16) for ref in kt_refs], [ref[0].astype(BF16) for ref in vt_refs], None)

    @pl.when(g == pl.num_programs(1) - 1)
    def _():
        o = jnp.where(own, jnp.transpose(acc_ref[...]), 0.0).reshape(t_new, n_heads, width)
        o_ref[0] = jnp.sum(o, axis=1)


def _sb_sample(page_table, bias, q_rep, k_new_t, v_new_t, cache_kt, cache_vt, layer_base, n_pg, t_new):
    b, rows, width = q_rep.shape
    page = cache_kt.shape[2]
    n_pages = page_table.shape[0] // b
    steps = n_pages // n_pg
    n_heads = rows // t_new

    def page_spec(p):
        def imap(bi, g, pt):
            return (layer_base + pt[bi * n_pages + (steps - 1 - g) * n_pg + p], 0, 0)
        return pl.BlockSpec((1, width, page), imap)

    per_b = lambda shape: pl.BlockSpec(shape, lambda bi, g, pt: (bi, 0, 0))
    grid_spec = pltpu.PrefetchScalarGridSpec(
        num_scalar_prefetch=1,
        grid=(b, steps),
        in_specs=[pl.BlockSpec(memory_space=pltpu.SMEM),
                  per_b((1, rows, width)), per_b((1, width, page)), per_b((1, width, page))]
        + [page_spec(p) for p in range(n_pg)] * 2,
        out_specs=per_b((1, t_new, width)),
        scratch_shapes=[pltpu.VMEM((width, rows), F32), pltpu.VMEM((rows, 1), F32)],
    )
    return pl.pallas_call(
        functools.partial(_sb_sample_kernel, n_pg=n_pg, n_heads=n_heads, t_new=t_new),
        grid_spec=grid_spec,
        out_shape=jax.ShapeDtypeStruct((b, t_new, width), F32),
        compiler_params=_cparams(2),
        name="sb_sample",
    )(page_table, bias, q_rep, k_new_t, v_new_t, *([cache_kt] * n_pg), *([cache_vt] * n_pg))


def _proj_ln_kernel(*refs, n_in, alpha):
    a_refs = refs[:n_in]
    w_refs = refs[n_in:2 * n_in]
    x_ref, g_ref, b_ref, o_ref = refs[2 * n_in:]
    y = alpha * x_ref[...]
    for a_ref, w_ref in zip(a_refs, w_refs):
        y = y + _dot(a_ref[...].astype(BF16), w_ref[...])
    o_ref[...] = _layer_norm(y, g_ref[...], b_ref[...])


def _proj_ln(a_list, w_list, x, g, b, alpha, tm):
    m, d = x.shape
    row = lambda width: pl.BlockSpec((tm, width), lambda i: (i, 0))
    return pl.pallas_call(
        functools.partial(_proj_ln_kernel, n_in=len(a_list), alpha=alpha),
        grid=(m // tm,),
        in_specs=[row(a.shape[1]) for a in a_list] + [_const_spec(w.shape) for w in w_list]
        + [row(d), _const_spec(g.shape), _const_spec(b.shape)],
        out_specs=row(d),
        out_shape=jax.ShapeDtypeStruct((m, d), F32),
        compiler_params=_cparams(1),
        name="proj_ln",
    )(*a_list, *w_list, x, g, b)


def _mem_head(q, k, v, scale):
    s = _dot_nt(q, k) * scale
    p = jnp.exp(s - jnp.max(s, axis=-1, keepdims=True))
    p = p * (1.0 / jnp.sum(p, axis=-1, keepdims=True))
    return _dot(p.astype(BF16), v)


def _mem_core_kernel(q_ref, k_ref, v_ref, o_ref, *, scale):
    hd = q_ref.shape[2] // N_MEM_HEADS
    for h in range(N_MEM_HEADS):
        cols = slice(h * hd, (h + 1) * hd)
        o_ref[0, :, cols] = _mem_head(q_ref[0, :, cols].astype(BF16), k_ref[0, :, cols].astype(BF16),
                                      v_ref[0, :, cols].astype(BF16), scale).astype(o_ref.dtype)


def _mix_mem_kernel(osb_ref, opool_ref, x_ref, wo_ref, g1_ref, b1_ref, wq_ref, k_ref, v_ref, wm_ref,
                    g2_ref, b2_ref, o_ref, *, alpha, scale):
    sbw = osb_ref.shape[2]
    y = alpha * x_ref[0] + _dot(osb_ref[0], wo_ref[:sbw]) + _dot(opool_ref[0], wo_ref[sbw:])
    x1 = _layer_norm(y, g1_ref[...], b1_ref[...])
    q = _dot(x1.astype(BF16), wq_ref[...]).astype(BF16)
    hd = q.shape[1] // N_MEM_HEADS
    y2 = alpha * x1
    for h in range(N_MEM_HEADS):
        cols = slice(h * hd, (h + 1) * hd)
        om = _mem_head(q[:, cols], k_ref[0, :, cols], v_ref[0, :, cols], scale).astype(BF16)
        y2 = y2 + _dot(om, wm_ref[cols, :])
    o_ref[0] = _layer_norm(y2, g2_ref[...], b2_ref[...])


def _mix_mem(o_sb, o_pool, x, w_out, g1, b1, w_mq, mem_k, mem_v, w_mo, g2, b2, alpha, tm):
    b, t, d = x.shape
    n_mem = mem_k.shape[1]
    scale = 1.0 / math.sqrt(d // N_MEM_HEADS)
    rows = lambda width: pl.BlockSpec((1, tm, width), lambda i, j: (i, j, 0))
    per_b = pl.BlockSpec((1, n_mem, d), lambda i, j: (i, 0, 0))
    const = lambda a: _const_spec(a.shape)
    return pl.pallas_call(
        functools.partial(_mix_mem_kernel, alpha=alpha, scale=scale),
        grid=(b, t // tm),
        in_specs=[rows(o_sb.shape[2]), rows(o_pool.shape[2]), rows(d), const(w_out), const(g1), const(b1),
                  const(w_mq), per_b, per_b, const(w_mo), const(g2), const(b2)],
        out_specs=rows(d),
        out_shape=jax.ShapeDtypeStruct((b, t, d), F32),
        compiler_params=_cparams(2),
        name="mix_mem",
    )(o_sb, o_pool, x, w_out, g1, b1, w_mq, mem_k, mem_v, w_mo, g2, b2)


def _mem_core(q, mem_k, mem_v, tm, base=0):
    b, t, d = q.shape
    n_mem = mem_k.shape[1]
    scale = 1.0 / math.sqrt(d // N_MEM_HEADS)
    return pl.pallas_call(
        functools.partial(_mem_core_kernel, scale=scale),
        grid=(b, t // tm),
        in_specs=[pl.BlockSpec((1, tm, d), lambda i, j: (i, j, 0)),
                  pl.BlockSpec((1, n_mem, d), lambda i, j: (base + i, 0, 0)),
                  pl.BlockSpec((1, n_mem, d), lambda i, j: (base + i, 0, 0))],
        out_specs=pl.BlockSpec((1, tm, d), lambda i, j: (i, j, 0)),
        out_shape=jax.ShapeDtypeStruct((b, t, d), q.dtype),
        compiler_params=_cparams(2),
        name="mem_core",
    )(q, mem_k, mem_v)


def _gelu_tanh(x):
    return 0.5 * x * (1.0 + jnp.tanh(math.sqrt(2.0 / math.pi) * (x + 0.044715 * (x * x * x))))


def _ffn_kernel(*refs, seq, d_ff, ck, alpha):
    if seq is None:
        x_ref, wu_ref, cw_ref, cb_ref, wd_ref, g_ref, b_ref, st_ref, o_ref, ns_ref, prev_ref = refs
        x = x_ref[0]
    else:
        x_ref, wu_ref, cw_ref, cb_ref, wd_ref, g_ref, b_ref, p1_ref, p2_ref, o_ref, h_ref = refs
        x = x_ref[...]
    tm = x.shape[0]
    n_prev = CONV_WIDTH - 1
    row = lax.broadcasted_iota(jnp.int32, (tm, 1), 0)
    if seq is None:
        @pl.when(pl.program_id(1) == 0)
        def _():
            prev_ref[...] = st_ref[0]
    else:
        row = row % seq

    xb = x.astype(BF16)
    y = alpha * x
    n_chunks = d_ff // ck

    def up(c):
        out = []
        for part in range(2):
            cols = slice(part * d_ff + c * ck, part * d_ff + (c + 1) * ck)
            h = _dot(xb, wu_ref[:, cols])
            if seq is None:
                p1 = prev_ref[1:2, cols]
                p2 = jnp.where(row == 1, p1, prev_ref[0:1, cols])
                prev_ref[:, cols] = h[tm - n_prev:]
                ns_ref[0, :, cols] = h[tm - n_prev:]
            else:
                p1 = p1_ref[:, cols]
                p2 = p2_ref[:, cols]
                h_ref[:, cols] = h
            out.append((h, p1, p2, cols))
        return out

    nxt = up(0)
    for c in range(n_chunks):
        cur = nxt
        if c + 1 < n_chunks:
            nxt = up(c + 1)
        conv = []
        for h, p1, p2, cols in cur:
            h1 = jnp.where(row >= 1, pltpu.roll(h, 1, 0), p1)
            h2 = jnp.where(row >= 2, pltpu.roll(h, 2, 0), p2)
            conv.append(cb_ref[:, cols] + cw_ref[0:1, cols] * h2 + cw_ref[1:2, cols] * h1
                        + cw_ref[2:3, cols] * h)
        act = (_gelu_tanh(conv[0]) * conv[1]).astype(BF16)
        y = y + _dot(act, wd_ref[c * ck:(c + 1) * ck, :])
    out = _layer_norm(y, g_ref[...], b_ref[...])
    if seq is None:
        o_ref[0] = out
    else:
        o_ref[...] = out


def _ffn_prompt(x, w_up, conv_w, conv_b, w_down, g, b, state, alpha, tm, ck):
    bsz, t, d = x.shape
    d_ff = w_down.shape[0]
    consts = [w_up, conv_w, conv_b, w_down, g, b]
    return pl.pallas_call(
        functools.partial(_ffn_kernel, seq=None, d_ff=d_ff, ck=ck, alpha=alpha),
        grid=(bsz, t // tm),
        in_specs=[pl.BlockSpec((1, tm, d), lambda i, j: (i, j, 0))]
        + [_const_spec(a.shape) for a in consts]
        + [pl.BlockSpec((1,) + state.shape[1:], lambda i, j: (i, 0, 0))],
        out_specs=[pl.BlockSpec((1, tm, d), lambda i, j: (i, j, 0)),
                   pl.BlockSpec((1,) + state.shape[1:], lambda i, j: (i, 0, 0))],
        out_shape=[jax.ShapeDtypeStruct(x.shape, F32), jax.ShapeDtypeStruct(state.shape, F32)],
        scratch_shapes=[pltpu.VMEM(state.shape[1:], F32)],
        compiler_params=_cparams(2),
        name="ffn_prompt",
    )(x, *consts, state)


def _ffn_sample(x, w_up, conv_w, conv_b, w_down, g, b, prev1, prev2, seq, alpha, ck):
    m, d = x.shape
    d_ff = w_down.shape[0]
    args = [x, w_up, conv_w, conv_b, w_down, g, b, prev1, prev2]
    return pl.pallas_call(
        functools.partial(_ffn_kernel, seq=seq, d_ff=d_ff, ck=ck, alpha=alpha),
        grid=(1,),
        in_specs=[_const_spec(a.shape) for a in args],
        out_specs=[_const_spec((m, d)), _const_spec((m, 2 * d_ff))],
        out_shape=[jax.ShapeDtypeStruct((m, d), F32), jax.ShapeDtypeStruct((m, 2 * d_ff), F32)],
        compiler_params=_cparams(1),
        name="ffn_sample",
    )(*args)


def kernel(x_prompt, x_sample, mem_prompt, cache_k, cache_v, page_table, cache_mem_k, cache_mem_v, state_pool, state_conv, w_in, sb_bias, pool_w, pool_scale, w_out, ln_mix_g, ln_mix_b, w_mq, w_mk, w_mv, w_mo, ln_mem_g, ln_mem_b, w_up, conv_w, conv_b, w_down, ln_ffn_g, ln_ffn_b):
    depth = w_in.shape[0]
    bsz, seq, d = x_prompt.shape
    dbsz, dseq, _ = x_sample.shape
    sbw = d // 2
    n_heads = sbw // SB_HEAD_DIM
    n_mem = mem_prompt.shape[1]
    d_ff = w_down.shape[1]
    n_phys, page = cache_k.shape[1], cache_k.shape[2]
    alpha = (2 * depth) ** 0.25
    m_p = bsz * seq
    m_s = dbsz * dseq

    w_in_b, pool_w_b, w_out_b = w_in.astype(BF16), pool_w.astype(BF16), w_out.astype(BF16)
    w_mq_b, w_mo_b, w_up_b, w_down_b = (w_mq.astype(BF16), w_mo.astype(BF16), w_up.astype(BF16),
                                        w_down.astype(BF16))
    row2 = lambda a, l: a[l][None, :]

    w_mkv = jnp.concatenate([w_mk[l] for l in range(depth)] + [w_mv[l] for l in range(depth)],
                            axis=1).astype(BF16)
    mem_out = _mm_split(mem_prompt.reshape(bsz * n_mem, d), w_mkv, 2 * depth, tm=512)
    mem_f32, mem_b16 = mem_out[:2 * depth], mem_out[2 * depth:]

    def mixer_tail(l, x2d, o_sb, o_pool, mem_attend, tm, q_dtype):
        x1 = _proj_ln([o_sb, o_pool], [w_out_b[l, :sbw], w_out_b[l, sbw:]], x2d,
                      row2(ln_mix_g, l), row2(ln_mix_b, l), alpha, tm)
        qm = _mm(x1, w_mq_b[l], q_dtype, tm)
        om = mem_attend(qm)
        return _proj_ln([om], [w_mo_b[l]], x1, row2(ln_mem_g, l), row2(ln_mem_b, l), alpha, tm)

    xp = x_prompt
    kp, vp, pp, cp = [], [], [], []
    pool0 = jnp.zeros((bsz, POOL_HALO, d - sbw), F32)
    conv0 = jnp.zeros((bsz, CONV_WIDTH - 1, 2 * d_ff), F32)
    for l in range(depth):
        wl = w_in_b[l]
        q, k, kb, v, vb, o_pool, npool = _proj_in_t(
            xp, wl[:, :sbw], wl[:, sbw:2 * sbw].T, wl[:, 2 * sbw:3 * sbw].T, wl[:, 3 * sbw:],
            pool0, pool_w_b[l], row2(pool_scale, l), tm=512)
        o_sb = _sb_prompt(q, kb, vb, sb_bias[l], tq=512, tk=1024)
        x2 = _mix_mem(o_sb, o_pool, xp, w_out_b[l], row2(ln_mix_g, l), row2(ln_mix_b, l), w_mq_b[l],
                      mem_b16[l].reshape(bsz, n_mem, d), mem_b16[depth + l].reshape(bsz, n_mem, d),
                      w_mo_b[l], row2(ln_mem_g, l), row2(ln_mem_b, l), alpha, tm=512)
        xp, nconv = _ffn_prompt(x2, w_up_b[l], conv_w[l], row2(conv_b, l), w_down_b[l],
                                row2(ln_ffn_g, l), row2(ln_ffn_b, l), conv0, alpha, tm=512, ck=256)
        kp.append(k)
        vp.append(v)
        pp.append(npool[:, 1:])
        cp.append(nconv)
    heads_last = lambda ts: jnp.stack(ts).reshape(depth, bsz, n_heads, SB_HEAD_DIM, seq).transpose(0, 1, 4, 2, 3)
    mkp = [mem_f32[l].reshape(bsz, n_mem, N_MEM_HEADS, d // N_MEM_HEADS) for l in range(depth)]
    mvp = [mem_f32[depth + l].reshape(bsz, n_mem, N_MEM_HEADS, d // N_MEM_HEADS) for l in range(depth)]

    xs = x_sample.reshape(m_s, d)
    past = page_table.shape[1] * page
    pt_flat = page_table.reshape(-1)
    pages_t = lambda c: c.transpose(0, 1, 3, 4, 2).reshape(depth * n_phys, sbw, page)
    ckt, cvt = pages_t(cache_k), pages_t(cache_v)
    cmk = cache_mem_k.reshape(depth * dbsz, n_mem, d)
    cmv = cache_mem_v.reshape(depth * dbsz, n_mem, d)
    ks_, vs_, ps_, cs_ = [], [], [], []
    seq_pad = 8
    for l in range(depth):
        q, k, v, kb, vb, u = _proj_in(xs, w_in_b[l], tm=m_s)
        q_rep = jnp.repeat(q.reshape(dbsz, dseq, sbw), n_heads, axis=1)
        new_t = lambda a: jnp.pad(a.reshape(dbsz, dseq, sbw).transpose(0, 2, 1),
                                  ((0, 0), (0, 0), (0, page - dseq)))
        o_sb = _sb_sample(pt_flat, sb_bias[l], q_rep, new_t(kb), new_t(vb), ckt, cvt,
                          l * n_phys, n_pg=16, t_new=dseq)
        u_pad = jnp.pad(u.reshape(dbsz, dseq, d - sbw), ((0, 0), (0, seq_pad - dseq), (0, 0)))
        buf = jnp.pad(state_pool[l], ((0, 0), (1, 0), (0, 0)))
        o_pool, npool = _pool(u_pad, buf, pool_w_b[l], row2(pool_scale, l), tp=seq_pad, t_valid=dseq,
                              start_pos=past)
        o_pool = o_pool[:, :dseq].reshape(m_s, d - sbw)
        def mem_sample(qm, l=l):
            q_pad = jnp.pad(qm.reshape(dbsz, dseq, d), ((0, 0), (0, seq_pad - dseq), (0, 0)))
            om = _mem_core(q_pad, cmk, cmv, tm=seq_pad, base=l * dbsz)
            return om[:, :dseq].reshape(m_s, d)

        x2 = mixer_tail(l, xs, o_sb.reshape(m_s, sbw), o_pool, mem_sample, tm=m_s, q_dtype=F32)
        st = state_conv[l]
        zrow = jnp.zeros_like(st[:, :1])
        prev1 = jnp.concatenate([st[:, 1:2]] + [zrow] * (dseq - 1), axis=1).reshape(m_s, 2 * d_ff)
        prev2 = jnp.concatenate([st[:, 0:1], st[:, 1:2]] + [zrow] * (dseq - 2), axis=1).reshape(m_s, 2 * d_ff)
        xs, h = _ffn_sample(x2, w_up_b[l], conv_w[l], row2(conv_b, l), w_down_b[l],
                            row2(ln_ffn_g, l), row2(ln_ffn_b, l), prev1, prev2, dseq, alpha, ck=256)
        ks_.append(k.reshape(dbsz, dseq, n_heads, SB_HEAD_DIM))
        vs_.append(v.reshape(dbsz, dseq, n_heads, SB_HEAD_DIM))
        ps_.append(npool[:, 1:])
        cs_.append(h.reshape(dbsz, dseq, 2 * d_ff)[:, dseq - (CONV_WIDTH - 1):])

    return (xp.reshape(bsz, seq, d), xs.reshape(dbsz, dseq, d),
            heads_last(kp), heads_last(vp), jnp.stack(mkp), jnp.stack(mvp), jnp.stack(pp), jnp.stack(cp),
            jnp.stack(ks_), jnp.stack(vs_), jnp.stack(ps_), jnp.stack(cs_))
```

```python
import functools
import math

import jax
import jax.numpy as jnp
from jax import lax
from jax.experimental import pallas as pl
from jax.experimental.pallas import tpu as pltpu

F32 = jnp.float32
BF16 = jnp.bfloat16

SB_HEAD_DIM = 64
POOL_WINDOWS = (2, 4, 8, 16)
POOL_BUF = max(POOL_WINDOWS) - 1
POOL_HALO = POOL_BUF + 1
N_MEM_HEADS = 4
CONV_WIDTH = 3
LN_EPS = 1e-5
SB_SCALE = 1.0 / math.sqrt(SB_HEAD_DIM)
LANES = 128
CUM_BLOCK = 256
VMEM_LIMIT = 48 * 1024 * 1024


def _cparams(n_grid_dims):
    return pltpu.CompilerParams(
        dimension_semantics=("arbitrary",) * n_grid_dims,
        vmem_limit_bytes=VMEM_LIMIT)


def _const_spec(shape):
    nd = len(shape)
    return pl.BlockSpec(shape, lambda *_: (0,) * nd, pipeline_mode=pl.Buffered(1))


def _layer_norm(y, g, b):
    mu = jnp.mean(y, axis=-1, keepdims=True)
    yc = y - mu
    var = jnp.mean(yc * yc, axis=-1, keepdims=True)
    return yc * lax.rsqrt(var + LN_EPS) * g + b


def _dot(a, b):
    return jnp.dot(a, b, preferred_element_type=F32)


def _dot_nt(a, b):
    return lax.dot_general(a, b, (((1,), (1,)), ((), ())), preferred_element_type=F32)


def _mm_split_kernel(x_ref, w_ref, *o_refs, n_out, width):
    xb = x_ref[...].astype(BF16)
    for i in range(n_out):
        y = _dot(xb, w_ref[:, i * width:(i + 1) * width])
        o_refs[i][...] = y
        o_refs[n_out + i][...] = y.astype(BF16)


def _mm_split(x, w, n_out, tm):
    m, kdim = x.shape
    width = w.shape[1] // n_out
    o_spec = pl.BlockSpec((tm, width), lambda i: (i, 0))
    return pl.pallas_call(
        functools.partial(_mm_split_kernel, n_out=n_out, width=width),
        grid=(m // tm,),
        in_specs=[pl.BlockSpec((tm, kdim), lambda i: (i, 0)), _const_spec(w.shape)],
        out_specs=[o_spec] * (2 * n_out),
        out_shape=[jax.ShapeDtypeStruct((m, width), F32)] * n_out
        + [jax.ShapeDtypeStruct((m, width), BF16)] * n_out,
        compiler_params=_cparams(1),
        name="mm_split",
    )(x, w)


def _mm_kernel(x_ref, w_ref, o_ref):
    o_ref[...] = _dot(x_ref[...].astype(BF16), w_ref[...]).astype(o_ref.dtype)


def _mm(x, w, out_dtype, tm):
    m, kdim = x.shape
    n = w.shape[1]
    return pl.pallas_call(
        _mm_kernel,
        grid=(m // tm,),
        in_specs=[pl.BlockSpec((tm, kdim), lambda i: (i, 0)), _const_spec(w.shape)],
        out_specs=pl.BlockSpec((tm, n), lambda i: (i, 0)),
        out_shape=jax.ShapeDtypeStruct((m, n), out_dtype),
        compiler_params=_cparams(1),
        name="mm",
    )(x, w)


def _proj_in_kernel(x_ref, w_ref, q_ref, k_ref, v_ref, kb_ref, vb_ref, u_ref, *, sbw):
    xb = x_ref[...].astype(BF16)
    q = _dot(xb, w_ref[:, 0:sbw])
    q_ref[...] = (q * SB_SCALE).astype(BF16)
    k = _dot(xb, w_ref[:, sbw:2 * sbw])
    k_ref[...] = k
    kb_ref[...] = k.astype(BF16)
    v = _dot(xb, w_ref[:, 2 * sbw:3 * sbw])
    v_ref[...] = v
    vb_ref[...] = v.astype(BF16)
    u_ref[...] = _dot(xb, w_ref[:, 3 * sbw:])


def _proj_in(x, w, tm):
    m, d = x.shape
    sbw = d // 2
    pw = w.shape[1] - 3 * sbw
    row = lambda width: pl.BlockSpec((tm, width), lambda i: (i, 0))
    return pl.pallas_call(
        functools.partial(_proj_in_kernel, sbw=sbw),
        grid=(m // tm,),
        in_specs=[row(d), _const_spec(w.shape)],
        out_specs=[row(sbw), row(sbw), row(sbw), row(sbw), row(sbw), row(pw)],
        out_shape=[jax.ShapeDtypeStruct((m, sbw), BF16),
                   jax.ShapeDtypeStruct((m, sbw), F32),
                   jax.ShapeDtypeStruct((m, sbw), F32),
                   jax.ShapeDtypeStruct((m, sbw), BF16),
                   jax.ShapeDtypeStruct((m, sbw), BF16),
                   jax.ShapeDtypeStruct((m, pw), F32)],
        compiler_params=_cparams(1),
        name="proj_in",
    )(x, w)


def _proj_in_t_kernel(x_ref, wq_ref, wkt_ref, wvt_ref, wu_ref, buf_ref, pw_ref, ps_ref,
                      q_ref, kt_ref, ktb_ref, vt_ref, vtb_ref, op_ref, np_ref, halo_ref, *, tm):
    xb = x_ref[0].astype(BF16)
    q_ref[0] = (_dot(xb, wq_ref[...]) * SB_SCALE).astype(BF16)
    kt = _dot_nt(wkt_ref[...], xb)
    kt_ref[0] = kt
    ktb_ref[0] = kt.astype(BF16)
    vt = _dot_nt(wvt_ref[...], xb)
    vt_ref[0] = vt
    vtb_ref[0] = vt.astype(BF16)
    _pool_tile(_dot(xb, wu_ref[...]), buf_ref, pw_ref, ps_ref, op_ref, np_ref, halo_ref,
               tp=tm, t_valid=tm, start_pos=0)


def _proj_in_t(x, wq, wkt, wvt, wu, pool_buf, pool_w, pool_scale, tm):
    b, t, d = x.shape
    sbw = wq.shape[1]
    pw = wu.shape[1]
    rows = lambda width: pl.BlockSpec((1, tm, width), lambda i, j: (i, j, 0))
    cols = pl.BlockSpec((1, sbw, tm), lambda i, j: (i, 0, j))
    halo = pl.BlockSpec((1, POOL_HALO, pw), lambda i, j: (i, 0, 0))
    consts = (wq, wkt, wvt, wu)
    return pl.pallas_call(
        functools.partial(_proj_in_t_kernel, tm=tm),
        grid=(b, t // tm),
        in_specs=[rows(d)] + [_const_spec(w.shape) for w in consts]
        + [halo, _const_spec(pool_w.shape), _const_spec(pool_scale.shape)],
        out_specs=[rows(sbw), cols, cols, cols, cols, rows(pw), halo],
        out_shape=[jax.ShapeDtypeStruct((b, t, sbw), BF16),
                   jax.ShapeDtypeStruct((b, sbw, t), F32),
                   jax.ShapeDtypeStruct((b, sbw, t), BF16),
                   jax.ShapeDtypeStruct((b, sbw, t), F32),
                   jax.ShapeDtypeStruct((b, sbw, t), BF16),
                   jax.ShapeDtypeStruct((b, t, pw), BF16),
                   jax.ShapeDtypeStruct((b, POOL_HALO, pw), F32)],
        scratch_shapes=[pltpu.VMEM((POOL_HALO, pw), F32)],
        compiler_params=_cparams(2),
        name="proj_in_t",
    )(x, *consts, pool_buf, pool_w, pool_scale)


def _pool_kernel(u_ref, buf_ref, w_ref, s_ref, o_ref, np_ref, halo_ref, *, tp, t_valid, start_pos):
    _pool_tile(u_ref[0], buf_ref, w_ref, s_ref, o_ref, np_ref, halo_ref, tp=tp, t_valid=t_valid,
               start_pos=start_pos)


def _pool_tile(u, buf_ref, w_ref, s_ref, o_ref, np_ref, halo_ref, *, tp, t_valid, start_pos):
    t = pl.program_id(1)

    @pl.when(t == 0)
    def _():
        halo_ref[...] = buf_ref[0]

    ext = jnp.concatenate([halo_ref[...], u], axis=0)
    pos = start_pos + t * tp + lax.broadcasted_iota(jnp.int32, (tp, 1), 0)
    group = u.shape[1] // len(POOL_WINDOWS)
    for g, win in enumerate(POOL_WINDOWS):
        c0 = g * group
        s = ext[:, c0:c0 + group]
        span = 1
        while span < win:
            s = s + pltpu.roll(s, span, 0)
            span *= 2
        wsum = s[POOL_HALO:POOL_HALO + tp]
        inv_cnt = 1.0 / jnp.minimum(win, pos + 1).astype(F32)
        d = (wsum * inv_cnt - u[:, c0:c0 + group]).astype(BF16)
        y = _dot(d, w_ref[g]) * s_ref[:, c0:c0 + group]
        o_ref[0, :, c0:c0 + group] = y.astype(o_ref.dtype)
    tail = ext[t_valid:t_valid + POOL_HALO]
    halo_ref[...] = tail
    np_ref[0] = tail


def _pool(u, buf, w, scale, tp, t_valid, start_pos):
    b, t, width = u.shape
    return pl.pallas_call(
        functools.partial(_pool_kernel, tp=tp, t_valid=t_valid, start_pos=start_pos),
        grid=(b, t // tp),
        in_specs=[pl.BlockSpec((1, tp, width), lambda i, j: (i, j, 0)),
                  pl.BlockSpec((1, POOL_HALO, width), lambda i, j: (i, 0, 0)),
                  _const_spec(w.shape), _const_spec(scale.shape)],
        out_specs=[pl.BlockSpec((1, tp, width), lambda i, j: (i, j, 0)),
                   pl.BlockSpec((1, POOL_HALO, width), lambda i, j: (i, 0, 0))],
        out_shape=[jax.ShapeDtypeStruct((b, t, width), BF16),
                   jax.ShapeDtypeStruct((b, POOL_HALO, width), F32)],
        scratch_shapes=[pltpu.VMEM((POOL_HALO, width), F32)],
        compiler_params=_cparams(2),
        name="pool",
    )(u, buf, w, scale)


def _softplus(z, mask):
    neg_abs = lax.bitcast_convert_type(
        lax.bitcast_convert_type(z, jnp.uint32) | jnp.uint32(0x80000000), F32)
    sp = jnp.maximum(z, 0.0) + jnp.log(1.0 + jnp.exp(neg_abs))
    if mask is not None:
        sp = jnp.where(mask, sp, 0.0)
    return sp


def _suffix(sp, tri, carry, split):
    n = sp.shape[1]
    last = lax.broadcasted_iota(jnp.int32, (1, LANES), 1) == LANES - 1
    blocks = []
    tail = None
    for b in reversed(range(n // CUM_BLOCK)):
        spb = sp[:, b * CUM_BLOCK:(b + 1) * CUM_BLOCK]
        if tail is None:
            spb = jnp.concatenate([spb[:, :CUM_BLOCK - LANES],
                                   spb[:, CUM_BLOCK - LANES:] + jnp.where(last, carry, 0.0)], axis=1)
        hi = spb.astype(BF16)
        s = _dot(hi, tri)
        if split:
            s = s + _dot((spb - hi.astype(F32)).astype(BF16), tri)
        if tail is not None:
            s = s + tail
        tail = s[:, 0:1]
        blocks.append(s)
    full = blocks[0] if len(blocks) == 1 else jnp.concatenate(blocks[::-1], axis=1)
    return full, tail


def _weights(z, s, mask):
    w = jnp.exp(z - s)
    if mask is not None:
        w = jnp.where(mask, w, 0.0)
    return w.astype(BF16)


def _suffix_matrix(n):
    r = lax.broadcasted_iota(jnp.int32, (n, n), 0)
    c = lax.broadcasted_iota(jnp.int32, (n, n), 1)
    return (r >= c).astype(BF16)


def _sb_prompt_kernel(bias_ref, q_ref, kt_ref, vt_ref, o_ref, *, tq, tk):
    hp = pl.program_id(1)
    i = pl.program_id(2)
    lane = lax.broadcasted_iota(jnp.int32, (1, LANES), 1)
    first = lane < SB_HEAD_DIM
    top = lax.broadcasted_iota(jnp.int32, (LANES, 1), 0) < SB_HEAD_DIM
    q2 = q_ref[0]
    qz = jnp.zeros_like(q2)
    qm = (jnp.where(first, q2, qz), jnp.where(first, qz, q2))
    bias = (bias_ref[2 * hp], bias_ref[2 * hp + 1])
    tri = _suffix_matrix(CUM_BLOCK)
    r = lax.broadcasted_iota(jnp.int32, (tq, tq), 0)
    c = lax.broadcasted_iota(jnp.int32, (tq, tq), 1)
    causal = c < r

    def tile(start, width, carry):
        c0, c1, acc = carry
        start = pl.multiple_of(start, tq)
        kt2 = kt_ref[0, :, pl.ds(start, width)]
        vt2 = vt_ref[0, :, pl.ds(start, width)]
        vz = jnp.zeros_like(vt2)
        vm = (jnp.where(top, vt2, vz), jnp.where(top, vz, vt2))
        cs = (c0, c1)
        zs = [_dot(qm[h], kt2) + bias[h] for h in range(2)]
        sps = [_softplus(z, None) for z in zs]
        sfx = [_suffix(sps[h], tri, cs[h], split=False) for h in range(2)]
        ws = [_weights(zs[h], sfx[h][0], None) for h in range(2)]
        for h in range(2):
            acc = acc + _dot_nt(ws[h], vm[h])
        return sfx[0][1], sfx[1][1], acc

    def diagonal():
        half = tq // 2
        start = pl.multiple_of(i * tq, tq)
        kt2 = kt_ref[0, :, pl.ds(start, tq)]
        vt2 = vt_ref[0, :, pl.ds(start, tq)]
        vz = jnp.zeros_like(vt2)
        vm = (jnp.where(top, vt2, vz), jnp.where(top, vz, vt2))
        m_hi = causal[:half, :half]
        m_lo = causal[:, :half]
        zero_h = jnp.zeros((half, 1), F32)
        z_hi = [_dot(qm[h][half:], kt2[:, half:]) + bias[h] for h in range(2)]
        z_lo = [_dot(qm[h], kt2[:, :half]) + bias[h] for h in range(2)]
        sfx_hi = [_suffix(_softplus(z, m_hi), tri, zero_h, split=False) for z in z_hi]
        w_hi = [_weights(z_hi[h], sfx_hi[h][0], m_hi) for h in range(2)]
        sfx_lo = [_suffix(_softplus(z_lo[h], m_lo), tri, jnp.concatenate([zero_h, sfx_hi[h][1]], axis=0),
                          split=False) for h in range(2)]
        w_lo = [_weights(z_lo[h], sfx_lo[h][0], m_lo) for h in range(2)]
        acc_hi = _dot_nt(w_hi[0], vm[0][:, half:]) + _dot_nt(w_hi[1], vm[1][:, half:])
        acc = _dot_nt(w_lo[0], vm[0][:, :half]) + _dot_nt(w_lo[1], vm[1][:, :half])
        acc = acc + jnp.concatenate([jnp.zeros((half, LANES), F32), acc_hi], axis=0)
        return sfx_lo[0][1], sfx_lo[1][1], acc

    carry = diagonal()
    ratio = tk // tq
    rem = i % ratio
    carry = lax.fori_loop(0, rem, lambda n, cr: tile((i - 1 - n) * tq, tq, cr), carry)
    base = (i - rem) * tq
    carry = lax.fori_loop(0, (i - rem) // ratio, lambda n, cr: tile(base - (n + 1) * tk, tk, cr), carry)
    o_ref[0] = carry[2].astype(o_ref.dtype)


def _sb_prompt(q, kt, vt, bias, tq, tk):
    b, t, width = q.shape
    assert tk % tq == 0 and t % tq == 0
    return pl.pallas_call(
        functools.partial(_sb_prompt_kernel, tq=tq, tk=tk),
        grid=(b, width // LANES, t // tq),
        in_specs=[pl.BlockSpec(memory_space=pltpu.SMEM),
                  pl.BlockSpec((1, tq, LANES), lambda bi, hp, i: (bi, i, hp)),
                  pl.BlockSpec((1, LANES, t), lambda bi, hp, i: (bi, hp, 0)),
                  pl.BlockSpec((1, LANES, t), lambda bi, hp, i: (bi, hp, 0))],
        out_specs=pl.BlockSpec((1, tq, LANES), lambda bi, hp, i: (bi, i, hp)),
        out_shape=jax.ShapeDtypeStruct((b, t, width), BF16),
        compiler_params=_cparams(3),
        name="sb_prompt",
    )(bias, q, kt, vt)


def _sb_sample_kernel(pt_ref, bias_ref, q_ref, knt_ref, vnt_ref, *rest, n_pg, n_heads, t_new):
    kt_refs = rest[:n_pg]
    vt_refs = rest[n_pg:2 * n_pg]
    o_ref = rest[2 * n_pg]
    acc_ref, c_ref = rest[2 * n_pg + 1:]
    g = pl.program_id(1)
    rows = t_new * n_heads
    width = q_ref.shape[2]
    page = knt_ref.shape[2]
    row_head = lax.broadcasted_iota(jnp.int32, (rows, 1), 0) % n_heads
    lane_head = lax.broadcasted_iota(jnp.int32, (1, width), 1) // SB_HEAD_DIM
    own = row_head == lane_head
    q2 = q_ref[0]
    qm = jnp.where(own, q2, jnp.zeros_like(q2))
    bias = jnp.zeros((rows, 1), F32)
    for h in range(n_heads):
        bias = jnp.where(row_head == h, bias_ref[h], bias)
    tri = _suffix_matrix(CUM_BLOCK)

    def attend(kts, vts, mask):
        z = jnp.concatenate([_dot(qm, kt) for kt in kts], axis=1) + bias
        if z.shape[1] % CUM_BLOCK:
            pad = CUM_BLOCK - z.shape[1] % CUM_BLOCK
            z = jnp.concatenate([z, jnp.zeros((rows, pad), F32)], axis=1)
            mask = jnp.concatenate([mask, jnp.zeros((rows, pad), jnp.bool_)], axis=1)
        s, c_new = _suffix(_softplus(z, mask), tri, c_ref[...], split=True)
        w = _weights(z, s, mask)
        c_ref[...] = c_new
        acc = acc_ref[...]
        for p, vt in enumerate(vts):
            wt = jnp.transpose(w[:, p * page:(p + 1) * page].astype(F32)).astype(BF16)
            acc = acc + _dot(vt, wt)
        acc_ref[...] = acc

    @pl.when(g == 0)
    def _():
        acc_ref[...] = jnp.zeros_like(acc_ref)
        c_ref[...] = jnp.zeros_like(c_ref)
        s_idx = lax.broadcasted_iota(jnp.int32, (rows, page), 1)
        t_idx = lax.broadcasted_iota(jnp.int32, (rows, page), 0) // n_heads
        attend([knt_ref[0]], [vnt_ref[0]], s_idx < t_idx)

    attend([ref[0].astype(BF16) for ref in kt_refs], [ref[0].astype(BF16) for ref in vt_refs], None)

    @pl.when(g == pl.num_programs(1) - 1)
    def _():
        o = jnp.where(own, jnp.transpose(acc_ref[...]), 0.0).reshape(t_new, n_heads, width)
        o_ref[0] = jnp.sum(o, axis=1)


def _sb_sample(page_table, bias, q_rep, k_new_t, v_new_t, cache_kt, cache_vt, layer_base, n_pg, t_new):
    b, rows, width = q_rep.shape
    page = cache_kt.shape[2]
    n_pages = page_table.shape[0] // b
    steps = n_pages // n_pg
    n_heads = rows // t_new

    def page_spec(p):
        def imap(bi, g, pt):
            return (layer_base + pt[bi * n_pages + (steps - 1 - g) * n_pg + p], 0, 0)
        return pl.BlockSpec((1, width, page), imap)

    per_b = lambda shape: pl.BlockSpec(shape, lambda bi, g, pt: (bi, 0, 0))
    grid_spec = pltpu.PrefetchScalarGridSpec(
        num_scalar_prefetch=1,
        grid=(b, steps),
        in_specs=[pl.BlockSpec(memory_space=pltpu.SMEM),
                  per_b((1, rows, width)), per_b((1, width, page)), per_b((1, width, page))]
        + [page_spec(p) for p in range(n_pg)] * 2,
        out_specs=per_b((1, t_new, width)),
        scratch_shapes=[pltpu.VMEM((width, rows), F32), pltpu.VMEM((rows, 1), F32)],
    )
    return pl.pallas_call(
        functools.partial(_sb_sample_kernel, n_pg=n_pg, n_heads=n_heads, t_new=t_new),
        grid_spec=grid_spec,
        out_shape=jax.ShapeDtypeStruct((b, t_new, width), F32),
        compiler_params=_cparams(2),
        name="sb_sample",
    )(page_table, bias, q_rep, k_new_t, v_new_t, *([cache_kt] * n_pg), *([cache_vt] * n_pg))


def _proj_ln_kernel(*refs, n_in, alpha):
    a_refs = refs[:n_in]
    w_refs = refs[n_in:2 * n_in]
    x_ref, g_ref, b_ref, o_ref = refs[2 * n_in:]
    y = alpha * x_ref[...]
    for a_ref, w_ref in zip(a_refs, w_refs):
        y = y + _dot(a_ref[...].astype(BF16), w_ref[...])
    o_ref[...] = _layer_norm(y, g_ref[...], b_ref[...])


def _proj_ln(a_list, w_list, x, g, b, alpha, tm):
    m, d = x.shape
    row = lambda width: pl.BlockSpec((tm, width), lambda i: (i, 0))
    return pl.pallas_call(
        functools.partial(_proj_ln_kernel, n_in=len(a_list), alpha=alpha),
        grid=(m // tm,),
        in_specs=[row(a.shape[1]) for a in a_list] + [_const_spec(w.shape) for w in w_list]
        + [row(d), _const_spec(g.shape), _const_spec(b.shape)],
        out_specs=row(d),
        out_shape=jax.ShapeDtypeStruct((m, d), F32),
        compiler_params=_cparams(1),
        name="proj_ln",
    )(*a_list, *w_list, x, g, b)


def _mem_head(q, k, v, scale):
    s = _dot_nt(q, k) * scale
    p = jnp.exp(s - jnp.max(s, axis=-1, keepdims=True))
    p = p * (1.0 / jnp.sum(p, axis=-1, keepdims=True))
    return _dot(p.astype(BF16), v)


def _mem_core_kernel(q_ref, k_ref, v_ref, o_ref, *, scale):
    hd = q_ref.shape[2] // N_MEM_HEADS
    for h in range(N_MEM_HEADS):
        cols = slice(h * hd, (h + 1) * hd)
        o_ref[0, :, cols] = _mem_head(q_ref[0, :, cols].astype(BF16), k_ref[0, :, cols].astype(BF16),
                                      v_ref[0, :, cols].astype(BF16), scale).astype(o_ref.dtype)


def _post_attn_kernel(osb_ref, opool_ref, x_ref, wo_ref, g1_ref, b1_ref, wq_ref, k_ref, v_ref, wm_ref,
                      g2_ref, b2_ref, wu_ref, cw_ref, cb_ref, wd_ref, g3_ref, b3_ref, st_ref,
                      o_ref, ns_ref, prev_ref, *, alpha, scale, d_ff, ck):
    x2 = _mix_mem_tile(osb_ref, opool_ref, x_ref, wo_ref, g1_ref, b1_ref, wq_ref, k_ref, v_ref, wm_ref,
                       g2_ref, b2_ref, alpha=alpha, scale=scale)
    o_ref[0] = _ffn_tile(x2, wu_ref, cw_ref, cb_ref, wd_ref, g3_ref, b3_ref, seq=None, d_ff=d_ff, ck=ck,
                         alpha=alpha, st_ref=st_ref, ns_ref=ns_ref, prev_ref=prev_ref)


def _mix_mem_tile(osb_ref, opool_ref, x_ref, wo_ref, g1_ref, b1_ref, wq_ref, k_ref, v_ref, wm_ref,
                  g2_ref, b2_ref, *, alpha, scale):
    sbw = osb_ref.shape[2]
    y = alpha * x_ref[0] + _dot(osb_ref[0], wo_ref[:sbw]) + _dot(opool_ref[0], wo_ref[sbw:])
    x1 = _layer_norm(y, g1_ref[...], b1_ref[...])
    q = _dot(x1.astype(BF16), wq_ref[...]).astype(BF16)
    hd = q.shape[1] // N_MEM_HEADS
    y2 = alpha * x1
    for h in range(N_MEM_HEADS):
        cols = slice(h * hd, (h + 1) * hd)
        om = _mem_head(q[:, cols], k_ref[0, :, cols], v_ref[0, :, cols], scale).astype(BF16)
        y2 = y2 + _dot(om, wm_ref[cols, :])
    return _layer_norm(y2, g2_ref[...], b2_ref[...])


def _post_attn(o_sb, o_pool, x, w_out, g1, b1, w_mq, mem_k, mem_v, w_mo, g2, b2,
               w_up, conv_w, conv_b, w_down, g3, b3, state, alpha, tm, ck):
    b, t, d = x.shape
    n_mem = mem_k.shape[1]
    d_ff = w_down.shape[0]
    scale = 1.0 / math.sqrt(d // N_MEM_HEADS)
    rows = lambda width: pl.BlockSpec((1, tm, width), lambda i, j: (i, j, 0))
    per_b = lambda a: pl.BlockSpec((1,) + a.shape[1:], lambda i, j: (i, 0, 0))
    const = lambda a: _const_spec(a.shape)
    return pl.pallas_call(
        functools.partial(_post_attn_kernel, alpha=alpha, scale=scale, d_ff=d_ff, ck=ck),
        grid=(b, t // tm),
        in_specs=[rows(o_sb.shape[2]), rows(o_pool.shape[2]), rows(d), const(w_out), const(g1), const(b1),
                  const(w_mq), per_b(mem_k), per_b(mem_v), const(w_mo), const(g2), const(b2),
                  const(w_up), const(conv_w), const(conv_b), const(w_down), const(g3), const(b3), per_b(state)],
        out_specs=[rows(d), per_b(state)],
        out_shape=[jax.ShapeDtypeStruct((b, t, d), F32), jax.ShapeDtypeStruct(state.shape, F32)],
        scratch_shapes=[pltpu.VMEM(state.shape[1:], F32)],
        compiler_params=_cparams(2),
        name="post_attn",
    )(o_sb, o_pool, x, w_out, g1, b1, w_mq, mem_k, mem_v, w_mo, g2, b2,
      w_up, conv_w, conv_b, w_down, g3, b3, state)


def _mem_core(q, mem_k, mem_v, tm, base=0):
    b, t, d = q.shape
    n_mem = mem_k.shape[1]
    scale = 1.0 / math.sqrt(d // N_MEM_HEADS)
    return pl.pallas_call(
        functools.partial(_mem_core_kernel, scale=scale),
        grid=(b, t // tm),
        in_specs=[pl.BlockSpec((1, tm, d), lambda i, j: (i, j, 0)),
                  pl.BlockSpec((1, n_mem, d), lambda i, j: (base + i, 0, 0)),
                  pl.BlockSpec((1, n_mem, d), lambda i, j: (base + i, 0, 0))],
        out_specs=pl.BlockSpec((1, tm, d), lambda i, j: (i, j, 0)),
        out_shape=jax.ShapeDtypeStruct((b, t, d), q.dtype),
        compiler_params=_cparams(2),
        name="mem_core",
    )(q, mem_k, mem_v)


def _gelu_tanh(x):
    return 0.5 * x * (1.0 + jnp.tanh(math.sqrt(2.0 / math.pi) * (x + 0.044715 * (x * x * x))))


def _ffn_sample_kernel(x_ref, wu_ref, cw_ref, cb_ref, wd_ref, g_ref, b_ref, p1_ref, p2_ref, o_ref, h_ref, *,
                       seq, d_ff, ck, alpha):
    o_ref[...] = _ffn_tile(x_ref[...], wu_ref, cw_ref, cb_ref, wd_ref, g_ref, b_ref, seq=seq, d_ff=d_ff,
                           ck=ck, alpha=alpha, p1_ref=p1_ref, p2_ref=p2_ref, h_ref=h_ref)


def _ffn_tile(x, wu_ref, cw_ref, cb_ref, wd_ref, g_ref, b_ref, *, seq, d_ff, ck, alpha,
              st_ref=None, ns_ref=None, prev_ref=None, p1_ref=None, p2_ref=None, h_ref=None):
    tm = x.shape[0]
    n_prev = CONV_WIDTH - 1
    row = lax.broadcasted_iota(jnp.int32, (tm, 1), 0)
    if seq is None:
        @pl.when(pl.program_id(1) == 0)
        def _():
            prev_ref[...] = st_ref[0]
    else:
        row = row % seq

    xb = x.astype(BF16)
    y = alpha * x
    n_chunks = d_ff // ck

    def up(c):
        out = []
        for part in range(2):
            cols = slice(part * d_ff + c * ck, part * d_ff + (c + 1) * ck)
            h = _dot(xb, wu_ref[:, cols])
            if seq is None:
                p1 = prev_ref[1:2, cols]
                p2 = jnp.where(row == 1, p1, prev_ref[0:1, cols])
                prev_ref[:, cols] = h[tm - n_prev:]
                ns_ref[0, :, cols] = h[tm - n_prev:]
            else:
                p1 = p1_ref[:, cols]
                p2 = p2_ref[:, cols]
                h_ref[:, cols] = h
            out.append((h, p1, p2, cols))
        return out

    nxt = up(0)
    for c in range(n_chunks):
        cur = nxt
        if c + 1 < n_chunks:
            nxt = up(c + 1)
        conv = []
        for h, p1, p2, cols in cur:
            h1 = jnp.where(row >= 1, pltpu.roll(h, 1, 0), p1)
            h2 = jnp.where(row >= 2, pltpu.roll(h, 2, 0), p2)
            conv.append(cb_ref[:, cols] + cw_ref[0:1, cols] * h2 + cw_ref[1:2, cols] * h1
                        + cw_ref[2:3, cols] * h)
        act = (_gelu_tanh(conv[0]) * conv[1]).astype(BF16)
        y = y + _dot(act, wd_ref[c * ck:(c + 1) * ck, :])
    return _layer_norm(y, g_ref[...], b_ref[...])


def _ffn_sample(x, w_up, conv_w, conv_b, w_down, g, b, prev1, prev2, seq, alpha, ck):
    m, d = x.shape
    d_ff = w_down.shape[0]
    args = [x, w_up, conv_w, conv_b, w_down, g, b, prev1, prev2]
    return pl.pallas_call(
        functools.partial(_ffn_sample_kernel, seq=seq, d_ff=d_ff, ck=ck, alpha=alpha),
        grid=(1,),
        in_specs=[_const_spec(a.shape) for a in args],
        out_specs=[_const_spec((m, d)), _const_spec((m, 2 * d_ff))],
        out_shape=[jax.ShapeDtypeStruct((m, d), F32), jax.ShapeDtypeStruct((m, 2 * d_ff), F32)],
        compiler_params=_cparams(1),
        name="ffn_sample",
    )(*args)


def kernel(x_prompt, x_sample, mem_prompt, cache_k, cache_v, page_table, cache_mem_k, cache_mem_v, state_pool, state_conv, w_in, sb_bias, pool_w, pool_scale, w_out, ln_mix_g, ln_mix_b, w_mq, w_mk, w_mv, w_mo, ln_mem_g, ln_mem_b, w_up, conv_w, conv_b, w_down, ln_ffn_g, ln_ffn_b):
    depth = w_in.shape[0]
    bsz, seq, d = x_prompt.shape
    dbsz, dseq, _ = x_sample.shape
    sbw = d // 2
    n_heads = sbw // SB_HEAD_DIM
    n_mem = mem_prompt.shape[1]
    d_ff = w_down.shape[1]
    n_phys, page = cache_k.shape[1], cache_k.shape[2]
    alpha = (2 * depth) ** 0.25
    m_p = bsz * seq
    m_s = dbsz * dseq

    w_in_b, pool_w_b, w_out_b = w_in.astype(BF16), pool_w.astype(BF16), w_out.astype(BF16)
    w_mq_b, w_mo_b, w_up_b, w_down_b = (w_mq.astype(BF16), w_mo.astype(BF16), w_up.astype(BF16),
                                        w_down.astype(BF16))
    row2 = lambda a, l: a[l][None, :]

    w_mkv = jnp.concatenate([w_mk[l] for l in range(depth)] + [w_mv[l] for l in range(depth)],
                            axis=1).astype(BF16)
    mem_out = _mm_split(mem_prompt.reshape(bsz * n_mem, d), w_mkv, 2 * depth, tm=512)
    mem_f32, mem_b16 = mem_out[:2 * depth], mem_out[2 * depth:]

    def mixer_tail(l, x2d, o_sb, o_pool, mem_attend, tm, q_dtype):
        x1 = _proj_ln([o_sb, o_pool], [w_out_b[l, :sbw], w_out_b[l, sbw:]], x2d,
                      row2(ln_mix_g, l), row2(ln_mix_b, l), alpha, tm)
        qm = _mm(x1, w_mq_b[l], q_dtype, tm)
        om = mem_attend(qm)
        return _proj_ln([om], [w_mo_b[l]], x1, row2(ln_mem_g, l), row2(ln_mem_b, l), alpha, tm)

    xp = x_prompt
    kp, vp, pp, cp = [], [], [], []
    pool0 = jnp.zeros((bsz, POOL_HALO, d - sbw), F32)
    conv0 = jnp.zeros((bsz, CONV_WIDTH - 1, 2 * d_ff), F32)
    for l in range(depth):
        wl = w_in_b[l]
        q, k, kb, v, vb, o_pool, npool = _proj_in_t(
            xp, wl[:, :sbw], wl[:, sbw:2 * sbw].T, wl[:, 2 * sbw:3 * sbw].T, wl[:, 3 * sbw:],
            pool0, pool_w_b[l], row2(pool_scale, l), tm=512)
        o_sb = _sb_prompt(q, kb, vb, sb_bias[l], tq=512, tk=1024)
        xp, nconv = _post_attn(o_sb, o_pool, xp, w_out_b[l], row2(ln_mix_g, l), row2(ln_mix_b, l), w_mq_b[l],
                               mem_b16[l].reshape(bsz, n_mem, d), mem_b16[depth + l].reshape(bsz, n_mem, d),
                               w_mo_b[l], row2(ln_mem_g, l), row2(ln_mem_b, l),
                               w_up_b[l], conv_w[l], row2(conv_b, l), w_down_b[l],
                               row2(ln_ffn_g, l), row2(ln_ffn_b, l), conv0, alpha, tm=512, ck=256)
        kp.append(k)
        vp.append(v)
        pp.append(npool[:, 1:])
        cp.append(nconv)
    heads_last = lambda ts: jnp.stack(ts).reshape(depth, bsz, n_heads, SB_HEAD_DIM, seq).transpose(0, 1, 4, 2, 3)
    mkp = [mem_f32[l].reshape(bsz, n_mem, N_MEM_HEADS, d // N_MEM_HEADS) for l in range(depth)]
    mvp = [mem_f32[depth + l].reshape(bsz, n_mem, N_MEM_HEADS, d // N_MEM_HEADS) for l in range(depth)]

    xs = x_sample.reshape(m_s, d)
    past = page_table.shape[1] * page
    pt_flat = page_table.reshape(-1)
    pages_t = lambda c: c.transpose(0, 1, 3, 4, 2).reshape(depth * n_phys, sbw, page)
    ckt, cvt = pages_t(cache_k), pages_t(cache_v)
    cmk = cache_mem_k.reshape(depth * dbsz, n_mem, d)
    cmv = cache_mem_v.reshape(depth * dbsz, n_mem, d)
    ks_, vs_, ps_, cs_ = [], [], [], []
    seq_pad = 8
    for l in range(depth):
        q, k, v, kb, vb, u = _proj_in(xs, w_in_b[l], tm=m_s)
        q_rep = jnp.repeat(q.reshape(dbsz, dseq, sbw), n_heads, axis=1)
        new_t = lambda a: jnp.pad(a.reshape(dbsz, dseq, sbw).transpose(0, 2, 1),
                                  ((0, 0), (0, 0), (0, page - dseq)))
        o_sb = _sb_sample(pt_flat, sb_bias[l], q_rep, new_t(kb), new_t(vb), ckt, cvt,
                          l * n_phys, n_pg=32, t_new=dseq)
        u_pad = jnp.pad(u.reshape(dbsz, dseq, d - sbw), ((0, 0), (0, seq_pad - dseq), (0, 0)))
        buf = jnp.pad(state_pool[l], ((0, 0), (1, 0), (0, 0)))
        o_pool, npool = _pool(u_pad, buf, pool_w_b[l], row2(pool_scale, l), tp=seq_pad, t_valid=dseq,
                              start_pos=past)
        o_pool = o_pool[:, :dseq].reshape(m_s, d - sbw)
        def mem_sample(qm, l=l):
            q_pad = jnp.pad(qm.reshape(dbsz, dseq, d), ((0, 0), (0, seq_pad - dseq), (0, 0)))
            om = _mem_core(q_pad, cmk, cmv, tm=seq_pad, base=l * dbsz)
            return om[:, :dseq].reshape(m_s, d)

        x2 = mixer_tail(l, xs, o_sb.reshape(m_s, sbw), o_pool, mem_sample, tm=m_s, q_dtype=F32)
        st = state_conv[l]
        zrow = jnp.zeros_like(st[:, :1])
        prev1 = jnp.concatenate([st[:, 1:2]] + [zrow] * (dseq - 1), axis=1).reshape(m_s, 2 * d_ff)
        prev2 = jnp.concatenate([st[:, 0:1], st[:, 1:2]] + [zrow] * (dseq - 2), axis=1).reshape(m_s, 2 * d_ff)
        xs, h = _ffn_sample(x2, w_up_b[l], conv_w[l], row2(conv_b, l), w_down_b[l],
                            row2(ln_ffn_g, l), row2(ln_ffn_b, l), prev1, prev2, dseq, alpha, ck=256)
        ks_.append(k.reshape(dbsz, dseq, n_heads, SB_HEAD_DIM))
        vs_.append(v.reshape(dbsz, dseq, n_heads, SB_HEAD_DIM))
        ps_.append(npool[:, 1:])
        cs_.append(h.reshape(dbsz, dseq, 2 * d_ff)[:, dseq - (CONV_WIDTH - 1):])

    return (xp.reshape(bsz, seq, d), xs.reshape(dbsz, dseq, d),
            heads_last(kp), heads_last(vp), jnp.stack(mkp), jnp.stack(mvp), jnp.stack(pp), jnp.stack(cp),
            jnp.stack(ks_), jnp.stack(vs_), jnp.stack(ps_), jnp.stack(cs_))
```

```python
import functools
import math

import jax
import jax.numpy as jnp
from jax import lax
from jax.experimental import pallas as pl
from jax.experimental.pallas import tpu as pltpu

F32 = jnp.float32
BF16 = jnp.bfloat16

SB_HEAD_DIM = 64
POOL_WINDOWS = (2, 4, 8, 16)
POOL_BUF = max(POOL_WINDOWS) - 1
POOL_HALO = POOL_BUF + 1
N_MEM_HEADS = 4
CONV_WIDTH = 3
LN_EPS = 1e-5
SB_SCALE = 1.0 / math.sqrt(SB_HEAD_DIM)
LANES = 128
CUM_BLOCK = 256
VMEM_LIMIT = 48 * 1024 * 1024


def _cparams(n_grid_dims):
    return pltpu.CompilerParams(
        dimension_semantics=("arbitrary",) * n_grid_dims,
        vmem_limit_bytes=VMEM_LIMIT)


def _const_spec(shape):
    nd = len(shape)
    return pl.BlockSpec(shape, lambda *_: (0,) * nd, pipeline_mode=pl.Buffered(1))


def _layer_norm(y, g, b):
    mu = jnp.mean(y, axis=-1, keepdims=True)
    yc = y - mu
    var = jnp.mean(yc * yc, axis=-1, keepdims=True)
    return yc * lax.rsqrt(var + LN_EPS) * g + b


def _dot(a, b):
    return jnp.dot(a, b, preferred_element_type=F32)


def _dot_nt(a, b):
    return lax.dot_general(a, b, (((1,), (1,)), ((), ())), preferred_element_type=F32)


def _mm_split_kernel(x_ref, w_ref, *o_refs, n_out, width):
    xb = x_ref[...].astype(BF16)
    for i in range(n_out):
        y = _dot(xb, w_ref[:, i * width:(i + 1) * width])
        o_refs[i][...] = y
        o_refs[n_out + i][...] = y.astype(BF16)


def _mm_split(x, w, n_out, tm):
    m, kdim = x.shape
    width = w.shape[1] // n_out
    o_spec = pl.BlockSpec((tm, width), lambda i: (i, 0))
    return pl.pallas_call(
        functools.partial(_mm_split_kernel, n_out=n_out, width=width),
        grid=(m // tm,),
        in_specs=[pl.BlockSpec((tm, kdim), lambda i: (i, 0)), _const_spec(w.shape)],
        out_specs=[o_spec] * (2 * n_out),
        out_shape=[jax.ShapeDtypeStruct((m, width), F32)] * n_out
        + [jax.ShapeDtypeStruct((m, width), BF16)] * n_out,
        compiler_params=_cparams(1),
        name="mm_split",
    )(x, w)


def _mm_kernel(x_ref, w_ref, o_ref):
    o_ref[...] = _dot(x_ref[...].astype(BF16), w_ref[...]).astype(o_ref.dtype)


def _mm(x, w, out_dtype, tm):
    m, kdim = x.shape
    n = w.shape[1]
    return pl.pallas_call(
        _mm_kernel,
        grid=(m // tm,),
        in_specs=[pl.BlockSpec((tm, kdim), lambda i: (i, 0)), _const_spec(w.shape)],
        out_specs=pl.BlockSpec((tm, n), lambda i: (i, 0)),
        out_shape=jax.ShapeDtypeStruct((m, n), out_dtype),
        compiler_params=_cparams(1),
        name="mm",
    )(x, w)


def _proj_in_kernel(x_ref, w_ref, q_ref, k_ref, v_ref, kb_ref, vb_ref, u_ref, *, sbw):
    xb = x_ref[...].astype(BF16)
    q = _dot(xb, w_ref[:, 0:sbw])
    q_ref[...] = (q * SB_SCALE).astype(BF16)
    k = _dot(xb, w_ref[:, sbw:2 * sbw])
    k_ref[...] = k
    kb_ref[...] = k.astype(BF16)
    v = _dot(xb, w_ref[:, 2 * sbw:3 * sbw])
    v_ref[...] = v
    vb_ref[...] = v.astype(BF16)
    u_ref[...] = _dot(xb, w_ref[:, 3 * sbw:])


def _proj_in(x, w, tm):
    m, d = x.shape
    sbw = d // 2
    pw = w.shape[1] - 3 * sbw
    row = lambda width: pl.BlockSpec((tm, width), lambda i: (i, 0))
    return pl.pallas_call(
        functools.partial(_proj_in_kernel, sbw=sbw),
        grid=(m // tm,),
        in_specs=[row(d), _const_spec(w.shape)],
        out_specs=[row(sbw), row(sbw), row(sbw), row(sbw), row(sbw), row(pw)],
        out_shape=[jax.ShapeDtypeStruct((m, sbw), BF16),
                   jax.ShapeDtypeStruct((m, sbw), F32),
                   jax.ShapeDtypeStruct((m, sbw), F32),
                   jax.ShapeDtypeStruct((m, sbw), BF16),
                   jax.ShapeDtypeStruct((m, sbw), BF16),
                   jax.ShapeDtypeStruct((m, pw), F32)],
        compiler_params=_cparams(1),
        name="proj_in",
    )(x, w)


def _proj_in_t_kernel(x_ref, wq_ref, wkt_ref, wvt_ref, wu_ref, buf_ref, pw_ref, ps_ref,
                      q_ref, kt_ref, ktb_ref, vt_ref, vtb_ref, op_ref, np_ref, halo_ref, *, tm):
    xb = x_ref[0].astype(BF16)
    q_ref[0] = (_dot(xb, wq_ref[...]) * SB_SCALE).astype(BF16)
    kt = _dot_nt(wkt_ref[...], xb)
    kt_ref[0] = kt
    ktb_ref[0] = kt.astype(BF16)
    vt = _dot_nt(wvt_ref[...], xb)
    vt_ref[0] = vt
    vtb_ref[0] = vt.astype(BF16)
    _pool_tile(_dot(xb, wu_ref[...]), buf_ref, pw_ref, ps_ref, op_ref, np_ref, halo_ref,
               tp=tm, t_valid=tm, start_pos=0)


def _proj_in_t(x, wq, wkt, wvt, wu, pool_buf, pool_w, pool_scale, tm):
    b, t, d = x.shape
    sbw = wq.shape[1]
    pw = wu.shape[1]
    rows = lambda width: pl.BlockSpec((1, tm, width), lambda i, j: (i, j, 0))
    cols = pl.BlockSpec((1, sbw, tm), lambda i, j: (i, 0, j))
    halo = pl.BlockSpec((1, POOL_HALO, pw), lambda i, j: (i, 0, 0))
    consts = (wq, wkt, wvt, wu)
    return pl.pallas_call(
        functools.partial(_proj_in_t_kernel, tm=tm),
        grid=(b, t // tm),
        in_specs=[rows(d)] + [_const_spec(w.shape) for w in consts]
        + [halo, _const_spec(pool_w.shape), _const_spec(pool_scale.shape)],
        out_specs=[rows(sbw), cols, cols, cols, cols, rows(pw), halo],
        out_shape=[jax.ShapeDtypeStruct((b, t, sbw), BF16),
                   jax.ShapeDtypeStruct((b, sbw, t), F32),
                   jax.ShapeDtypeStruct((b, sbw, t), BF16),
                   jax.ShapeDtypeStruct((b, sbw, t), F32),
                   jax.ShapeDtypeStruct((b, sbw, t), BF16),
                   jax.ShapeDtypeStruct((b, t, pw), BF16),
                   jax.ShapeDtypeStruct((b, POOL_HALO, pw), F32)],
        scratch_shapes=[pltpu.VMEM((POOL_HALO, pw), F32)],
        compiler_params=_cparams(2),
        name="proj_in_t",
    )(x, *consts, pool_buf, pool_w, pool_scale)


def _pool_kernel(u_ref, buf_ref, w_ref, s_ref, o_ref, np_ref, halo_ref, *, tp, t_valid, start_pos):
    _pool_tile(u_ref[0], buf_ref, w_ref, s_ref, o_ref, np_ref, halo_ref, tp=tp, t_valid=t_valid,
               start_pos=start_pos)


def _pool_tile(u, buf_ref, w_ref, s_ref, o_ref, np_ref, halo_ref, *, tp, t_valid, start_pos):
    t = pl.program_id(1)

    @pl.when(t == 0)
    def _():
        halo_ref[...] = buf_ref[0]

    ext = jnp.concatenate([halo_ref[...], u], axis=0)
    pos = start_pos + t * tp + lax.broadcasted_iota(jnp.int32, (tp, 1), 0)
    group = u.shape[1] // len(POOL_WINDOWS)
    for g, win in enumerate(POOL_WINDOWS):
        c0 = g * group
        s = ext[:, c0:c0 + group]
        span = 1
        while span < win:
            s = s + pltpu.roll(s, span, 0)
            span *= 2
        wsum = s[POOL_HALO:POOL_HALO + tp]
        inv_cnt = 1.0 / jnp.minimum(win, pos + 1).astype(F32)
        d = (wsum * inv_cnt - u[:, c0:c0 + group]).astype(BF16)
        y = _dot(d, w_ref[g]) * s_ref[:, c0:c0 + group]
        o_ref[0, :, c0:c0 + group] = y.astype(o_ref.dtype)
    tail = ext[t_valid:t_valid + POOL_HALO]
    halo_ref[...] = tail
    np_ref[0] = tail


def _pool(u, buf, w, scale, tp, t_valid, start_pos):
    b, t, width = u.shape
    return pl.pallas_call(
        functools.partial(_pool_kernel, tp=tp, t_valid=t_valid, start_pos=start_pos),
        grid=(b, t // tp),
        in_specs=[pl.BlockSpec((1, tp, width), lambda i, j: (i, j, 0)),
                  pl.BlockSpec((1, POOL_HALO, width), lambda i, j: (i, 0, 0)),
                  _const_spec(w.shape), _const_spec(scale.shape)],
        out_specs=[pl.BlockSpec((1, tp, width), lambda i, j: (i, j, 0)),
                   pl.BlockSpec((1, POOL_HALO, width), lambda i, j: (i, 0, 0))],
        out_shape=[jax.ShapeDtypeStruct((b, t, width), BF16),
                   jax.ShapeDtypeStruct((b, POOL_HALO, width), F32)],
        scratch_shapes=[pltpu.VMEM((POOL_HALO, width), F32)],
        compiler_params=_cparams(2),
        name="pool",
    )(u, buf, w, scale)


def _softplus(z, mask):
    neg_abs = lax.bitcast_convert_type(
        lax.bitcast_convert_type(z, jnp.uint32) | jnp.uint32(0x80000000), F32)
    sp = jnp.maximum(z, 0.0) + jnp.log(1.0 + jnp.exp(neg_abs))
    if mask is not None:
        sp = jnp.where(mask, sp, 0.0)
    return sp


def _suffix(sp, tri, carry, split):
    n = sp.shape[1]
    last = lax.broadcasted_iota(jnp.int32, (1, LANES), 1) == LANES - 1
    blocks = []
    tail = None
    for b in reversed(range(n // CUM_BLOCK)):
        spb = sp[:, b * CUM_BLOCK:(b + 1) * CUM_BLOCK]
        if tail is None:
            spb = jnp.concatenate([spb[:, :CUM_BLOCK - LANES],
                                   spb[:, CUM_BLOCK - LANES:] + jnp.where(last, carry, 0.0)], axis=1)
        hi = spb.astype(BF16)
        s = _dot(hi, tri)
        if split:
            s = s + _dot((spb - hi.astype(F32)).astype(BF16), tri)
        if tail is not None:
            s = s + tail
        tail = s[:, 0:1]
        blocks.append(s)
    full = blocks[0] if len(blocks) == 1 else jnp.concatenate(blocks[::-1], axis=1)
    return full, tail


def _weights(z, s, mask):
    w = jnp.exp(z - s)
    if mask is not None:
        w = jnp.where(mask, w, 0.0)
    return w.astype(BF16)


def _suffix_matrix(n):
    r = lax.broadcasted_iota(jnp.int32, (n, n), 0)
    c = lax.broadcasted_iota(jnp.int32, (n, n), 1)
    return (r >= c).astype(BF16)


def _sb_prompt_kernel(bias_ref, q_ref, kt_ref, vt_ref, o_ref, kaug_ref, *, tq, tk):
    hp = pl.program_id(1)
    i = pl.program_id(2)
    lane = lax.broadcasted_iota(jnp.int32, (1, LANES), 1)
    first = lane < SB_HEAD_DIM
    row = lax.broadcasted_iota(jnp.int32, (LANES, 1), 0)
    top = row < SB_HEAD_DIM

    @pl.when(i == 0)
    def _():
        kt = kt_ref[0].astype(F32)
        kaug_ref[0] = jnp.where(top, kt, jnp.where(row < SB_HEAD_DIM + 2, 1.0, 0.0)).astype(BF16)
        kaug_ref[1] = jnp.where(top, jnp.where(row < 2, 1.0, 0.0), kt).astype(BF16)

    qf = q_ref[0].astype(F32)
    qm = []
    for h in range(2):
        offset = jnp.full((1, LANES), bias_ref[2 * hp + h], F32)
        hi = offset.astype(BF16).astype(F32)
        at = SB_HEAD_DIM if h == 0 else 0
        extra = jnp.where(lane == at, hi, jnp.where(lane == at + 1, offset - hi, 0.0))
        qm.append(jnp.where(first if h == 0 else jnp.logical_not(first), qf, extra).astype(BF16))
    tri = _suffix_matrix(CUM_BLOCK)
    r = lax.broadcasted_iota(jnp.int32, (tq, tq), 0)
    c = lax.broadcasted_iota(jnp.int32, (tq, tq), 1)
    causal = c < r

    def tile(start, width, carry):
        c0, c1, acc = carry
        start = pl.multiple_of(start, tq)
        vt2 = vt_ref[0, :, pl.ds(start, width)]
        vz = jnp.zeros_like(vt2)
        vm = (jnp.where(top, vt2, vz), jnp.where(top, vz, vt2))
        cs = (c0, c1)
        zs = [_dot(qm[h], kaug_ref[h, :, pl.ds(start, width)]) for h in range(2)]
        sps = [_softplus(z, None) for z in zs]
        sfx = [_suffix(sps[h], tri, cs[h], split=False) for h in range(2)]
        ws = [_weights(zs[h], sfx[h][0], None) for h in range(2)]
        for h in range(2):
            acc = acc + _dot_nt(ws[h], vm[h])
        return sfx[0][1], sfx[1][1], acc

    def diagonal():
        half = tq // 2
        start = pl.multiple_of(i * tq, tq)
        vt2 = vt_ref[0, :, pl.ds(start, tq)]
        vz = jnp.zeros_like(vt2)
        vm = (jnp.where(top, vt2, vz), jnp.where(top, vz, vt2))
        m_hi = causal[:half, :half]
        m_lo = causal[:, :half]
        zero_h = jnp.zeros((half, 1), F32)
        z_hi = [_dot(qm[h][half:], kaug_ref[h, :, pl.ds(start + half, half)]) for h in range(2)]
        z_lo = [_dot(qm[h], kaug_ref[h, :, pl.ds(start, half)]) for h in range(2)]
        sfx_hi = [_suffix(_softplus(z, m_hi), tri, zero_h, split=False) for z in z_hi]
        w_hi = [_weights(z_hi[h], sfx_hi[h][0], m_hi) for h in range(2)]
        sfx_lo = [_suffix(_softplus(z_lo[h], m_lo), tri, jnp.concatenate([zero_h, sfx_hi[h][1]], axis=0),
                          split=False) for h in range(2)]
        w_lo = [_weights(z_lo[h], sfx_lo[h][0], m_lo) for h in range(2)]
        acc_hi = _dot_nt(w_hi[0], vm[0][:, half:]) + _dot_nt(w_hi[1], vm[1][:, half:])
        acc = _dot_nt(w_lo[0], vm[0][:, :half]) + _dot_nt(w_lo[1], vm[1][:, :half])
        acc = acc + jnp.concatenate([jnp.zeros((half, LANES), F32), acc_hi], axis=0)
        return sfx_lo[0][1], sfx_lo[1][1], acc

    carry = diagonal()
    ratio = tk // tq
    rem = i % ratio
    carry = lax.fori_loop(0, rem, lambda n, cr: tile((i - 1 - n) * tq, tq, cr), carry)
    base = (i - rem) * tq
    carry = lax.fori_loop(0, (i - rem) // ratio, lambda n, cr: tile(base - (n + 1) * tk, tk, cr), carry)
    o_ref[0] = carry[2].astype(o_ref.dtype)


def _sb_prompt(q, kt, vt, bias, tq, tk):
    b, t, width = q.shape
    assert tk % tq == 0 and t % tq == 0
    return pl.pallas_call(
        functools.partial(_sb_prompt_kernel, tq=tq, tk=tk),
        grid=(b, width // LANES, t // tq),
        in_specs=[pl.BlockSpec(memory_space=pltpu.SMEM),
                  pl.BlockSpec((1, tq, LANES), lambda bi, hp, i: (bi, i, hp)),
                  pl.BlockSpec((1, LANES, t), lambda bi, hp, i: (bi, hp, 0)),
                  pl.BlockSpec((1, LANES, t), lambda bi, hp, i: (bi, hp, 0))],
        out_specs=pl.BlockSpec((1, tq, LANES), lambda bi, hp, i: (bi, i, hp)),
        out_shape=jax.ShapeDtypeStruct((b, t, width), BF16),
        scratch_shapes=[pltpu.VMEM((2, LANES, t), BF16)],
        compiler_params=_cparams(3),
        name="sb_prompt",
    )(bias, q, kt, vt)


def _sb_sample_kernel(pt_ref, bias_ref, q_ref, knt_ref, vnt_ref, *rest, n_pg, n_heads, t_new):
    kt_refs = rest[:n_pg]
    vt_refs = rest[n_pg:2 * n_pg]
    o_ref = rest[2 * n_pg]
    acc_ref, c_ref = rest[2 * n_pg + 1:]
    g = pl.program_id(1)
    rows = t_new * n_heads
    width = q_ref.shape[2]
    page = knt_ref.shape[2]
    row_head = lax.broadcasted_iota(jnp.int32, (rows, 1), 0) % n_heads
    lane_head = lax.broadcasted_iota(jnp.int32, (1, width), 1) // SB_HEAD_DIM
    own = row_head == lane_head
    q2 = q_ref[0]
    qm = jnp.where(own, q2, jnp.zeros_like(q2))
    bias = jnp.zeros((rows, 1), F32)
    for h in range(n_heads):
        bias = jnp.where(row_head == h, bias_ref[h], bias)
    tri = _suffix_matrix(CUM_BLOCK)

    def attend(kts, vts, mask):
        z = jnp.concatenate([_dot(qm, kt) for kt in kts], axis=1) + bias
        if z.shape[1] % CUM_BLOCK:
            pad = CUM_BLOCK - z.shape[1] % CUM_BLOCK
            z = jnp.concatenate([z, jnp.zeros((rows, pad), F32)], axis=1)
            mask = jnp.concatenate([mask, jnp.zeros((rows, pad), jnp.bool_)], axis=1)
        s, c_new = _suffix(_softplus(z, mask), tri, c_ref[...], split=True)
        w = _weights(z, s, mask)
        c_ref[...] = c_new
        acc = acc_ref[...]
        for p, vt in enumerate(vts):
            wt = jnp.transpose(w[:, p * page:(p + 1) * page].astype(F32)).astype(BF16)
            acc = acc + _dot(vt, wt)
        acc_ref[...] = acc

    @pl.when(g == 0)
    def _():
        acc_ref[...] = jnp.zeros_like(acc_ref)
        c_ref[...] = jnp.zeros_like(c_ref)
        s_idx = lax.broadcasted_iota(jnp.int32, (rows, page), 1)
        t_idx = lax.broadcasted_iota(jnp.int32, (rows, page), 0) // n_heads
        attend([knt_ref[0]], [vnt_ref[0]], s_idx < t_idx)

    attend([ref[0].astype(BF16) for ref in kt_refs], [ref[0].astype(BF16) for ref in vt_refs], None)

    @pl.when(g == pl.num_programs(1) - 1)
    def _():
        o = jnp.where(own, jnp.transpose(acc_ref[...]), 0.0).reshape(t_new, n_heads, width)
        o_ref[0] = jnp.sum(o, axis=1)


def _sb_sample(page_table, bias, q_rep, k_new_t, v_new_t, cache_kt, cache_vt, layer_base, n_pg, t_new):
    b, rows, width = q_rep.shape
    page = cache_kt.shape[2]
    n_pages = page_table.shape[0] // b
    steps = n_pages // n_pg
    n_heads = rows // t_new

    def page_spec(p):
        def imap(bi, g, pt):
            return (layer_base + pt[bi * n_pages + (steps - 1 - g) * n_pg + p], 0, 0)
        return pl.BlockSpec((1, width, page), imap)

    per_b = lambda shape: pl.BlockSpec(shape, lambda bi, g, pt: (bi, 0, 0))
    grid_spec = pltpu.PrefetchScalarGridSpec(
        num_scalar_prefetch=1,
        grid=(b, steps),
        in_specs=[pl.BlockSpec(memory_space=pltpu.SMEM),
                  per_b((1, rows, width)), per_b((1, width, page)), per_b((1, width, page))]
        + [page_spec(p) for p in range(n_pg)] * 2,
        out_specs=per_b((1, t_new, width)),
        scratch_shapes=[pltpu.VMEM((width, rows), F32), pltpu.VMEM((rows, 1), F32)],
    )
    return pl.pallas_call(
        functools.partial(_sb_sample_kernel, n_pg=n_pg, n_heads=n_heads, t_new=t_new),
        grid_spec=grid_spec,
        out_shape=jax.ShapeDtypeStruct((b, t_new, width), F32),
        compiler_params=_cparams(2),
        name="sb_sample",
    )(page_table, bias, q_rep, k_new_t, v_new_t, *([cache_kt] * n_pg), *([cache_vt] * n_pg))


def _proj_ln_kernel(*refs, n_in, alpha):
    a_refs = refs[:n_in]
    w_refs = refs[n_in:2 * n_in]
    x_ref, g_ref, b_ref, o_ref = refs[2 * n_in:]
    y = alpha * x_ref[...]
    for a_ref, w_ref in zip(a_refs, w_refs):
        y = y + _dot(a_ref[...].astype(BF16), w_ref[...])
    o_ref[...] = _layer_norm(y, g_ref[...], b_ref[...])


def _proj_ln(a_list, w_list, x, g, b, alpha, tm):
    m, d = x.shape
    row = lambda width: pl.BlockSpec((tm, width), lambda i: (i, 0))
    return pl.pallas_call(
        functools.partial(_proj_ln_kernel, n_in=len(a_list), alpha=alpha),
        grid=(m // tm,),
        in_specs=[row(a.shape[1]) for a in a_list] + [_const_spec(w.shape) for w in w_list]
        + [row(d), _const_spec(g.shape), _const_spec(b.shape)],
        out_specs=row(d),
        out_shape=jax.ShapeDtypeStruct((m, d), F32),
        compiler_params=_cparams(1),
        name="proj_ln",
    )(*a_list, *w_list, x, g, b)


def _mem_head(q, k, v, scale):
    s = _dot_nt(q, k) * scale
    p = jnp.exp(s - jnp.max(s, axis=-1, keepdims=True))
    p = p * (1.0 / jnp.sum(p, axis=-1, keepdims=True))
    return _dot(p.astype(BF16), v)


def _mem_core_kernel(q_ref, k_ref, v_ref, o_ref, *, scale):
    hd = q_ref.shape[2] // N_MEM_HEADS
    for h in range(N_MEM_HEADS):
        cols = slice(h * hd, (h + 1) * hd)
        o_ref[0, :, cols] = _mem_head(q_ref[0, :, cols].astype(BF16), k_ref[0, :, cols].astype(BF16),
                                      v_ref[0, :, cols].astype(BF16), scale).astype(o_ref.dtype)


def _post_attn_kernel(osb_ref, opool_ref, x_ref, wo_ref, g1_ref, b1_ref, wq_ref, k_ref, v_ref, wm_ref,
                      g2_ref, b2_ref, wu_ref, cw_ref, cb_ref, wd_ref, g3_ref, b3_ref, st_ref,
                      o_ref, ns_ref, prev_ref, *, alpha, scale, d_ff, ck):
    x2 = _mix_mem_tile(osb_ref, opool_ref, x_ref, wo_ref, g1_ref, b1_ref, wq_ref, k_ref, v_ref, wm_ref,
                       g2_ref, b2_ref, alpha=alpha, scale=scale)
    o_ref[0] = _ffn_tile(x2, wu_ref, cw_ref, cb_ref, wd_ref, g3_ref, b3_ref, seq=None, d_ff=d_ff, ck=ck,
                         alpha=alpha, st_ref=st_ref, ns_ref=ns_ref, prev_ref=prev_ref)


def _mix_mem_tile(osb_ref, opool_ref, x_ref, wo_ref, g1_ref, b1_ref, wq_ref, k_ref, v_ref, wm_ref,
                  g2_ref, b2_ref, *, alpha, scale):
    sbw = osb_ref.shape[2]
    y = alpha * x_ref[0] + _dot(osb_ref[0], wo_ref[:sbw]) + _dot(opool_ref[0], wo_ref[sbw:])
    x1 = _layer_norm(y, g1_ref[...], b1_ref[...])
    q = _dot(x1.astype(BF16), wq_ref[...]).astype(BF16)
    hd = q.shape[1] // N_MEM_HEADS
    y2 = alpha * x1
    for h in range(N_MEM_HEADS):
        cols = slice(h * hd, (h + 1) * hd)
        om = _mem_head(q[:, cols], k_ref[0, :, cols], v_ref[0, :, cols], scale).astype(BF16)
        y2 = y2 + _dot(om, wm_ref[cols, :])
    return _layer_norm(y2, g2_ref[...], b2_ref[...])


def _post_attn(o_sb, o_pool, x, w_out, g1, b1, w_mq, mem_k, mem_v, w_mo, g2, b2,
               w_up, conv_w, conv_b, w_down, g3, b3, state, alpha, tm, ck):
    b, t, d = x.shape
    n_mem = mem_k.shape[1]
    d_ff = w_down.shape[0]
    scale = 1.0 / math.sqrt(d // N_MEM_HEADS)
    rows = lambda width: pl.BlockSpec((1, tm, width), lambda i, j: (i, j, 0))
    per_b = lambda a: pl.BlockSpec((1,) + a.shape[1:], lambda i, j: (i, 0, 0))
    const = lambda a: _const_spec(a.shape)
    return pl.pallas_call(
        functools.partial(_post_attn_kernel, alpha=alpha, scale=scale, d_ff=d_ff, ck=ck),
        grid=(b, t // tm),
        in_specs=[rows(o_sb.shape[2]), rows(o_pool.shape[2]), rows(d), const(w_out), const(g1), const(b1),
                  const(w_mq), per_b(mem_k), per_b(mem_v), const(w_mo), const(g2), const(b2),
                  const(w_up), const(conv_w), const(conv_b), const(w_down), const(g3), const(b3), per_b(state)],
        out_specs=[rows(d), per_b(state)],
        out_shape=[jax.ShapeDtypeStruct((b, t, d), F32), jax.ShapeDtypeStruct(state.shape, F32)],
        scratch_shapes=[pltpu.VMEM(state.shape[1:], F32)],
        compiler_params=_cparams(2),
        name="post_attn",
    )(o_sb, o_pool, x, w_out, g1, b1, w_mq, mem_k, mem_v, w_mo, g2, b2,
      w_up, conv_w, conv_b, w_down, g3, b3, state)


def _mem_core(q, mem_k, mem_v, tm, base=0):
    b, t, d = q.shape
    n_mem = mem_k.shape[1]
    scale = 1.0 / math.sqrt(d // N_MEM_HEADS)
    return pl.pallas_call(
        functools.partial(_mem_core_kernel, scale=scale),
        grid=(b, t // tm),
        in_specs=[pl.BlockSpec((1, tm, d), lambda i, j: (i, j, 0)),
                  pl.BlockSpec((1, n_mem, d), lambda i, j: (base + i, 0, 0)),
                  pl.BlockSpec((1, n_mem, d), lambda i, j: (base + i, 0, 0))],
        out_specs=pl.BlockSpec((1, tm, d), lambda i, j: (i, j, 0)),
        out_shape=jax.ShapeDtypeStruct((b, t, d), q.dtype),
        compiler_params=_cparams(2),
        name="mem_core",
    )(q, mem_k, mem_v)


def _gelu_tanh(x):
    return 0.5 * x * (1.0 + jnp.tanh(math.sqrt(2.0 / math.pi) * (x + 0.044715 * (x * x * x))))


def _ffn_sample_kernel(x_ref, wu_ref, cw_ref, cb_ref, wd_ref, g_ref, b_ref, p1_ref, p2_ref, o_ref, h_ref, *,
                       seq, d_ff, ck, alpha):
    o_ref[...] = _ffn_tile(x_ref[...], wu_ref, cw_ref, cb_ref, wd_ref, g_ref, b_ref, seq=seq, d_ff=d_ff,
                           ck=ck, alpha=alpha, p1_ref=p1_ref, p2_ref=p2_ref, h_ref=h_ref)


def _ffn_tile(x, wu_ref, cw_ref, cb_ref, wd_ref, g_ref, b_ref, *, seq, d_ff, ck, alpha,
              st_ref=None, ns_ref=None, prev_ref=None, p1_ref=None, p2_ref=None, h_ref=None):
    tm = x.shape[0]
    n_prev = CONV_WIDTH - 1
    row = lax.broadcasted_iota(jnp.int32, (tm, 1), 0)
    if seq is None:
        @pl.when(pl.program_id(1) == 0)
        def _():
            prev_ref[...] = st_ref[0]
    else:
        row = row % seq

    xb = x.astype(BF16)
    y = alpha * x
    n_chunks = d_ff // ck

    def up(c):
        out = []
        for part in range(2):
            cols = slice(part * d_ff + c * ck, part * d_ff + (c + 1) * ck)
            h = _dot(xb, wu_ref[:, cols])
            if seq is None:
                p1 = prev_ref[1:2, cols]
                p2 = jnp.where(row == 1, p1, prev_ref[0:1, cols])
                prev_ref[:, cols] = h[tm - n_prev:]
                ns_ref[0, :, cols] = h[tm - n_prev:]
            else:
                p1 = p1_ref[:, cols]
                p2 = p2_ref[:, cols]
                h_ref[:, cols] = h
            out.append((h, p1, p2, cols))
        return out

    nxt = up(0)
    for c in range(n_chunks):
        cur = nxt
        if c + 1 < n_chunks:
            nxt = up(c + 1)
        conv = []
        for h, p1, p2, cols in cur:
            h1 = jnp.where(row >= 1, pltpu.roll(h, 1, 0), p1)
            h2 = jnp.where(row >= 2, pltpu.roll(h, 2, 0), p2)
            conv.append(cb_ref[:, cols] + cw_ref[0:1, cols] * h2 + cw_ref[1:2, cols] * h1
                        + cw_ref[2:3, cols] * h)
        act = (_gelu_tanh(conv[0]) * conv[1]).astype(BF16)
        y = y + _dot(act, wd_ref[c * ck:(c + 1) * ck, :])
    return _layer_norm(y, g_ref[...], b_ref[...])


def _ffn_sample(x, w_up, conv_w, conv_b, w_down, g, b, prev1, prev2, seq, alpha, ck):
    m, d = x.shape
    d_ff = w_down.shape[0]
    args = [x, w_up, conv_w, conv_b, w_down, g, b, prev1, prev2]
    return pl.pallas_call(
        functools.partial(_ffn_sample_kernel, seq=seq, d_ff=d_ff, ck=ck, alpha=alpha),
        grid=(1,),
        in_specs=[_const_spec(a.shape) for a in args],
        out_specs=[_const_spec((m, d)), _const_spec((m, 2 * d_ff))],
        out_shape=[jax.ShapeDtypeStruct((m, d), F32), jax.ShapeDtypeStruct((m, 2 * d_ff), F32)],
        compiler_params=_cparams(1),
        name="ffn_sample",
    )(*args)


def kernel(x_prompt, x_sample, mem_prompt, cache_k, cache_v, page_table, cache_mem_k, cache_mem_v, state_pool, state_conv, w_in, sb_bias, pool_w, pool_scale, w_out, ln_mix_g, ln_mix_b, w_mq, w_mk, w_mv, w_mo, ln_mem_g, ln_mem_b, w_up, conv_w, conv_b, w_down, ln_ffn_g, ln_ffn_b):
    depth = w_in.shape[0]
    bsz, seq, d = x_prompt.shape
    dbsz, dseq, _ = x_sample.shape
    sbw = d // 2
    n_heads = sbw // SB_HEAD_DIM
    n_mem = mem_prompt.shape[1]
    d_ff = w_down.shape[1]
    n_phys, page = cache_k.shape[1], cache_k.shape[2]
    alpha = (2 * depth) ** 0.25
    m_p = bsz * seq
    m_s = dbsz * dseq

    w_in_b, pool_w_b, w_out_b = w_in.astype(BF16), pool_w.astype(BF16), w_out.astype(BF16)
    w_mq_b, w_mo_b, w_up_b, w_down_b = (w_mq.astype(BF16), w_mo.astype(BF16), w_up.astype(BF16),
                                        w_down.astype(BF16))
    row2 = lambda a, l: a[l][None, :]

    w_mkv = jnp.concatenate([w_mk[l] for l in range(depth)] + [w_mv[l] for l in range(depth)],
                            axis=1).astype(BF16)
    mem_out = _mm_split(mem_prompt.reshape(bsz * n_mem, d), w_mkv, 2 * depth, tm=512)
    mem_f32, mem_b16 = mem_out[:2 * depth], mem_out[2 * depth:]

    def mixer_tail(l, x2d, o_sb, o_pool, mem_attend, tm, q_dtype):
        x1 = _proj_ln([o_sb, o_pool], [w_out_b[l, :sbw], w_out_b[l, sbw:]], x2d,
                      row2(ln_mix_g, l), row2(ln_mix_b, l), alpha, tm)
        qm = _mm(x1, w_mq_b[l], q_dtype, tm)
        om = mem_attend(qm)
        return _proj_ln([om], [w_mo_b[l]], x1, row2(ln_mem_g, l), row2(ln_mem_b, l), alpha, tm)

    xp = x_prompt
    kp, vp, pp, cp = [], [], [], []
    pool0 = jnp.zeros((bsz, POOL_HALO, d - sbw), F32)
    conv0 = jnp.zeros((bsz, CONV_WIDTH - 1, 2 * d_ff), F32)
    for l in range(depth):
        wl = w_in_b[l]
        q, k, kb, v, vb, o_pool, npool = _proj_in_t(
            xp, wl[:, :sbw], wl[:, sbw:2 * sbw].T, wl[:, 2 * sbw:3 * sbw].T, wl[:, 3 * sbw:],
            pool0, pool_w_b[l], row2(pool_scale, l), tm=512)
        o_sb = _sb_prompt(q, kb, vb, sb_bias[l], tq=512, tk=1024)
        xp, nconv = _post_attn(o_sb, o_pool, xp, w_out_b[l], row2(ln_mix_g, l), row2(ln_mix_b, l), w_mq_b[l],
                               mem_b16[l].reshape(bsz, n_mem, d), mem_b16[depth + l].reshape(bsz, n_mem, d),
                               w_mo_b[l], row2(ln_mem_g, l), row2(ln_mem_b, l),
                               w_up_b[l], conv_w[l], row2(conv_b, l), w_down_b[l],
                               row2(ln_ffn_g, l), row2(ln_ffn_b, l), conv0, alpha, tm=512, ck=256)
        kp.append(k)
        vp.append(v)
        pp.append(npool[:, 1:])
        cp.append(nconv)
    heads_last = lambda ts: jnp.stack(ts).reshape(depth, bsz, n_heads, SB_HEAD_DIM, seq).transpose(0, 1, 4, 2, 3)
    mkp = [mem_f32[l].reshape(bsz, n_mem, N_MEM_HEADS, d // N_MEM_HEADS) for l in range(depth)]
    mvp = [mem_f32[depth + l].reshape(bsz, n_mem, N_MEM_HEADS, d // N_MEM_HEADS) for l in range(depth)]

    xs = x_sample.reshape(m_s, d)
    past = page_table.shape[1] * page
    pt_flat = page_table.reshape(-1)
    pages_t = lambda c: c.transpose(0, 1, 3, 4, 2).reshape(depth * n_phys, sbw, page)
    ckt, cvt = pages_t(cache_k), pages_t(cache_v)
    cmk = cache_mem_k.reshape(depth * dbsz, n_mem, d)
    cmv = cache_mem_v.reshape(depth * dbsz, n_mem, d)
    ks_, vs_, ps_, cs_ = [], [], [], []
    seq_pad = 8
    for l in range(depth):
        q, k, v, kb, vb, u = _proj_in(xs, w_in_b[l], tm=m_s)
        q_rep = jnp.repeat(q.reshape(dbsz, dseq, sbw), n_heads, axis=1)
        new_t = lambda a: jnp.pad(a.reshape(dbsz, dseq, sbw).transpose(0, 2, 1),
                                  ((0, 0), (0, 0), (0, page - dseq)))
        o_sb = _sb_sample(pt_flat, sb_bias[l], q_rep, new_t(kb), new_t(vb), ckt, cvt,
                          l * n_phys, n_pg=32, t_new=dseq)
        u_pad = jnp.pad(u.reshape(dbsz, dseq, d - sbw), ((0, 0), (0, seq_pad - dseq), (0, 0)))
        buf = jnp.pad(state_pool[l], ((0, 0), (1, 0), (0, 0)))
        o_pool, npool = _pool(u_pad, buf, pool_w_b[l], row2(pool_scale, l), tp=seq_pad, t_valid=dseq,
                              start_pos=past)
        o_pool = o_pool[:, :dseq].reshape(m_s, d - sbw)
        def mem_sample(qm, l=l):
            q_pad = jnp.pad(qm.reshape(dbsz, dseq, d), ((0, 0), (0, seq_pad - dseq), (0, 0)))
            om = _mem_core(q_pad, cmk, cmv, tm=seq_pad, base=l * dbsz)
            return om[:, :dseq].reshape(m_s, d)

        x2 = mixer_tail(l, xs, o_sb.reshape(m_s, sbw), o_pool, mem_sample, tm=m_s, q_dtype=F32)
        st = state_conv[l]
        zrow = jnp.zeros_like(st[:, :1])
        prev1 = jnp.concatenate([st[:, 1:2]] + [zrow] * (dseq - 1), axis=1).reshape(m_s, 2 * d_ff)
        prev2 = jnp.concatenate([st[:, 0:1], st[:, 1:2]] + [zrow] * (dseq - 2), axis=1).reshape(m_s, 2 * d_ff)
        xs, h = _ffn_sample(x2, w_up_b[l], conv_w[l], row2(conv_b, l), w_down_b[l],
                            row2(ln_ffn_g, l), row2(ln_ffn_b, l), prev1, prev2, dseq, alpha, ck=256)
        ks_.append(k.reshape(dbsz, dseq, n_heads, SB_HEAD_DIM))
        vs_.append(v.reshape(dbsz, dseq, n_heads, SB_HEAD_DIM))
        ps_.append(npool[:, 1:])
        cs_.append(h.reshape(dbsz, dseq, 2 * d_ff)[:, dseq - (CONV_WIDTH - 1):])

    return (xp.reshape(bsz, seq, d), xs.reshape(dbsz, dseq, d),
            heads_last(kp), heads_last(vp), jnp.stack(mkp), jnp.stack(mvp), jnp.stack(pp), jnp.stack(cp),
            jnp.stack(ks_), jnp.stack(vs_), jnp.stack(ps_), jnp.stack(cs_))
```
